```python
import math
import jax, jax.numpy as jnp
from jax import lax
import numpy as np

D_MODEL = 2048
BATCH = 4
SEQ = 4096
DEPTH = 2

LRU_WIDTH = D_MODEL
LRU_HEADS = 16
LRU_HEAD_DIM = LRU_WIDTH // LRU_HEADS
CONV_W = 4
LRU_C = 8.0
HEAD_DIM = 128
N_Q_HEADS = 16
N_KV_HEADS = 4
GQA_GROUP = N_Q_HEADS // N_KV_HEADS
WINDOW = 128
ATTN_WIDTH = N_Q_HEADS * HEAD_DIM
KV_WIDTH = N_KV_HEADS * HEAD_DIM
N_BRANCHES = 2
C_IN = 2 * LRU_WIDTH + ATTN_WIDTH + 2 * KV_WIDTH + N_BRANCHES * D_MODEL
SPLITS = tuple(np.cumsum([LRU_WIDTH, LRU_WIDTH, ATTN_WIDTH, KV_WIDTH, KV_WIDTH]).tolist())
FFN_HIDDEN = int(math.ceil(8 * D_MODEL / 3 / 256) * 256)
N_MOD = 6
EPS = 1e-6

kernel_name = "hybrid_rglru_swa_sink_adaln_block"


def rms_norm(x, g):
    xf = x.astype(jnp.float32)
    inv = lax.rsqrt(jnp.mean(xf * xf, axis=-1, keepdims=True) + EPS)
    return (xf * inv).astype(x.dtype) * g


def modulate(h, shift, scale):
    return h * (1 + scale) + shift


def causal_depthwise_conv(u, w, b):
    S = u.shape[1]
    up = jnp.pad(u, ((0, 0), (CONV_W - 1, 0), (0, 0)))
    out = b
    for k in range(CONV_W):
        out = out + up[:, k:k + S, :] * w[k]
    return out


def rg_lru(u, wa, ba, wx, bx, lam):
    B, S, C = u.shape
    uh = u.reshape(B, S, LRU_HEADS, LRU_HEAD_DIM)
    r = jax.nn.sigmoid(jnp.einsum('bshi,hij->bshj', uh, wa).reshape(B, S, C) + ba)
    i = jax.nn.sigmoid(jnp.einsum('bshi,hij->bshj', uh, wx).reshape(B, S, C) + bx)
    log_a = -LRU_C * r.astype(jnp.float32) * jax.nn.softplus(-lam.astype(jnp.float32))
    a = jnp.exp(log_a)
    beta = jnp.sqrt(-jnp.expm1(2.0 * log_a))
    inp = beta * (i * u).astype(jnp.float32)

    def combine(left, right):
        a1, b1 = left
        a2, b2 = right
        return a1 * a2, a2 * b1 + b2

    _, h = lax.associative_scan(combine, (a, inp), axis=1)
    return h.astype(u.dtype)


def sliding_window_attention(q, k, v, sinks):
    B, S = q.shape[0], q.shape[1]
    nb = S // WINDOW
    qb = q.reshape(B, nb, WINDOW, N_KV_HEADS, GQA_GROUP, HEAD_DIM)
    kb = k.reshape(B, nb, WINDOW, N_KV_HEADS, HEAD_DIM)
    vb = v.reshape(B, nb, WINDOW, N_KV_HEADS, HEAD_DIM)
    pad = ((0, 0), (1, 0), (0, 0), (0, 0), (0, 0))
    kw = jnp.concatenate([jnp.pad(kb, pad)[:, :-1], kb], axis=2)
    vw = jnp.concatenate([jnp.pad(vb, pad)[:, :-1], vb], axis=2)
    s = jnp.einsum('bnqhgd,bnkhd->bnhgqk', qb, kw).astype(jnp.float32) * (HEAD_DIM ** -0.5)
    qi = jnp.arange(WINDOW)[:, None]
    kk = jnp.arange(2 * WINDOW)[None, :]
    band = (kk > qi) & (kk <= qi + WINDOW)
    blk = jnp.arange(nb)[:, None, None]
    valid = band[None] & (blk * WINDOW + kk[None] - WINDOW >= 0)
    s = jnp.where(valid[None, :, None, None], s, -jnp.inf)
    sink = sinks.astype(jnp.float32).reshape(N_KV_HEADS, GQA_GROUP)[None, None, :, :, None, None]
    m = jnp.maximum(jnp.max(s, axis=-1, keepdims=True), sink)
    p = jnp.exp(s - m)
    denom = jnp.sum(p, axis=-1, keepdims=True) + jnp.exp(sink - m)
    p = (p / denom).astype(v.dtype)
    o = jnp.einsum('bnhgqk,bnkhd->bnqhgd', p, vw)
    return o.reshape(B, S, ATTN_WIDTH)


def setup_inputs(seed: int = 0) -> dict:
    key = jax.random.key(seed)
    ks = jax.random.split(key, 24)
    f32 = jnp.float32

    def nrm(k, shape, scale):
        return jax.random.normal(k, shape, f32) * scale

    a0 = jax.random.uniform(ks[11], (DEPTH, LRU_WIDTH), f32, 0.9, 0.999)
    p0 = a0 ** (1.0 / LRU_C)
    lru_lambda = jnp.log(p0) - jnp.log1p(-p0)
    return {
        "x": nrm(ks[0], (BATCH, SEQ, D_MODEL), 1.0),
        "c": nrm(ks[1], (BATCH, D_MODEL), 1.0),
        "ada_w": nrm(ks[2], (DEPTH, D_MODEL, N_MOD * D_MODEL), D_MODEL ** -0.5),
        "ada_b": nrm(ks[3], (DEPTH, N_MOD * D_MODEL), 0.01),
        "norm1_g": 1.0 + nrm(ks[4], (DEPTH, D_MODEL), 0.02),
        "w_in": nrm(ks[5], (DEPTH, D_MODEL, C_IN), D_MODEL ** -0.5),
        "b_in": nrm(ks[6], (DEPTH, C_IN), 0.01),
        "conv_w": nrm(ks[7], (DEPTH, CONV_W, LRU_WIDTH), CONV_W ** -0.5),
        "conv_b": nrm(ks[8], (DEPTH, LRU_WIDTH), 0.01),
        "lru_wa": nrm(ks[9], (DEPTH, LRU_HEADS, LRU_HEAD_DIM, LRU_HEAD_DIM), LRU_HEAD_DIM ** -0.5),
        "lru_ba": nrm(ks[10], (DEPTH, LRU_WIDTH), 0.01),
        "lru_wx": nrm(ks[12], (DEPTH, LRU_HEADS, LRU_HEAD_DIM, LRU_HEAD_DIM), LRU_HEAD_DIM ** -0.5),
        "lru_bx": nrm(ks[13], (DEPTH, LRU_WIDTH), 0.01),
        "lru_lambda": lru_lambda,
        "sinks": nrm(ks[14], (DEPTH, N_Q_HEADS), 0.5),
        "w_lru_out": nrm(ks[15], (DEPTH, LRU_WIDTH, D_MODEL), LRU_WIDTH ** -0.5),
        "w_attn_out": nrm(ks[16], (DEPTH, ATTN_WIDTH, D_MODEL), ATTN_WIDTH ** -0.5),
        "w_o": nrm(ks[17], (DEPTH, D_MODEL, D_MODEL), D_MODEL ** -0.5),
        "norm2_g": 1.0 + nrm(ks[18], (DEPTH, D_MODEL), 0.02),
        "w_ffn_in": nrm(ks[19], (DEPTH, D_MODEL, 2 * FFN_HIDDEN), D_MODEL ** -0.5),
        "w_ffn_out": nrm(ks[20], (DEPTH, FFN_HIDDEN, D_MODEL), FFN_HIDDEN ** -0.5),
        "final_g": 1.0 + nrm(ks[21], (D_MODEL,), 0.02),
    }


def reference(x, c, ada_w, ada_b, norm1_g, w_in, b_in, conv_w, conv_b, lru_wa, lru_ba,
              lru_wx, lru_bx, lru_lambda, sinks, w_lru_out, w_attn_out, w_o, norm2_g,
              w_ffn_in, w_ffn_out, final_g):
    B, S, _ = x.shape
    c_act = jax.nn.silu(c)
    for l in range(DEPTH):
        mod = (c_act @ ada_w[l] + ada_b[l])[:, None, :]
        sh1, sc1, g1, sh2, sc2, g2 = jnp.split(mod, N_MOD, axis=-1)

        h = modulate(rms_norm(x, norm1_g[l]), sh1, sc1)
        proj = h @ w_in[l] + b_in[l]
        u, lru_gate, q, k, v, mix_gates = jnp.split(proj, SPLITS, axis=-1)

        u = causal_depthwise_conv(u, conv_w[l], conv_b[l])
        y_lru = rg_lru(u, lru_wa[l], lru_ba[l], lru_wx[l], lru_bx[l], lru_lambda[l])
        y_lru = y_lru * jax.nn.gelu(lru_gate, approximate=True)

        y_attn = sliding_window_attention(
            q.reshape(B, S, N_Q_HEADS, HEAD_DIM),
            k.reshape(B, S, N_KV_HEADS, HEAD_DIM),
            v.reshape(B, S, N_KV_HEADS, HEAD_DIM),
            sinks[l])

        gate_lru, gate_attn = jnp.split(jax.nn.sigmoid(mix_gates), N_BRANCHES, axis=-1)
        merged = gate_lru * (y_lru @ w_lru_out[l]) + gate_attn * (y_attn @ w_attn_out[l])
        x = x + g1 * (merged @ w_o[l])

        h2 = modulate(rms_norm(x, norm2_g[l]), sh2, sc2)
        gate, up = jnp.split(h2 @ w_ffn_in[l], 2, axis=-1)
        x = x + g2 * ((jax.nn.silu(gate) * up) @ w_ffn_out[l])
    return rms_norm(x, final_g)
```

```python
import functools

import jax
import jax.numpy as jnp
from jax import lax
from jax.experimental import pallas as pl
from jax.experimental.pallas import tpu as pltpu

F32 = jnp.float32
BF16 = jnp.bfloat16

D_MODEL = 2048
LRU_HEADS = 16
LRU_HEAD_DIM = D_MODEL // LRU_HEADS
CONV_W = 4
LRU_C = 8.0
HEAD_DIM = 128
N_Q_HEADS = 16
N_KV_HEADS = 4
GQA_GROUP = N_Q_HEADS // N_KV_HEADS
WINDOW = 128
KV_WIDTH = N_KV_HEADS * HEAD_DIM
N_MOD = 6
EPS = 1e-6
C_IN = 5 * D_MODEL + 2 * KV_WIDTH
COL_U, COL_GATE, COL_Q, COL_GA, COL_GB = 0, 1, 2, 3, 4
COL_K = 5 * D_MODEL // KV_WIDTH
COL_V = COL_K + 1

SUBLANES = 8
VMEM_LIMIT = 56 * 1024 * 1024


def _params(*sem):
    return pltpu.CompilerParams(dimension_semantics=sem, vmem_limit_bytes=VMEM_LIMIT)


def _resident(shape, index_map):
    return pl.BlockSpec(shape, index_map, pipeline_mode=pl.Buffered(1))


def _norm_mod_rows(x, g, shift, scale):
    inv = lax.rsqrt(jnp.mean(x * x, axis=-1, keepdims=True) + EPS)
    return (x * inv) * g * (1.0 + scale) + shift


def _mod_kernel(c_ref, w_ref, b_ref, o_ref):
    c = c_ref[...]
    act = (c * jax.nn.sigmoid(c)).astype(BF16)
    o_ref[0] = jnp.dot(act, w_ref[0].astype(BF16), preferred_element_type=F32) + b_ref[0]


def _modulation(c, ada_w, ada_b, tn=1024):
    depth, d, n = ada_w.shape
    b = c.shape[0]
    rows = -(-b // SUBLANES) * SUBLANES
    c_pad = jnp.zeros((rows, d), F32).at[:b].set(c)
    out = pl.pallas_call(
        _mod_kernel,
        grid=(depth, n // tn),
        in_specs=[
            pl.BlockSpec((rows, d), lambda l, j: (0, 0)),
            pl.BlockSpec((1, d, tn), lambda l, j: (l, 0, j)),
            pl.BlockSpec((1, 1, tn), lambda l, j: (l, 0, j)),
        ],
        out_specs=pl.BlockSpec((1, rows, tn), lambda l, j: (l, 0, j)),
        out_shape=jax.ShapeDtypeStruct((depth, rows, n), F32),
        compiler_params=_params("arbitrary", "arbitrary"),
        name="adaln_modulation",
    )(c_pad, ada_w, ada_b.reshape(depth, 1, n))
    return out[:, :b].reshape(depth, b, N_MOD, d)


def _inproj_kernel(x_ref, mod_ref, g_ref, w_ref, b_ref, o_ref, h_ref, *, rows):
    tm = x_ref.shape[0]

    @pl.when(pl.program_id(1) == 0)
    def _():
        def body(i, carry):
            r = pl.ds(pl.multiple_of(i * rows, rows), rows)
            h = _norm_mod_rows(x_ref[r, :], g_ref[...], mod_ref[0, 0:1, :], mod_ref[0, 1:2, :])
            h_ref[r, :] = h.astype(BF16)
            return carry
        lax.fori_loop(0, tm // rows, body, 0)

    acc = jnp.dot(h_ref[...], w_ref[...], preferred_element_type=F32)
    o_ref[...] = (acc + b_ref[...]).astype(o_ref.dtype)


def _in_proj(x, mod, g, w, b, seq, tm, tn):
    t, d = x.shape
    n = w.shape[1]
    per_seq = seq // tm
    return pl.pallas_call(
        functools.partial(_inproj_kernel, rows=min(tm, 128)),
        grid=(t // tm, n // tn),
        in_specs=[
            pl.BlockSpec((tm, d), lambda i, j: (i, 0)),
            pl.BlockSpec((1, N_MOD, d), lambda i, j: (i // per_seq, 0, 0)),
            pl.BlockSpec((1, d), lambda i, j: (0, 0)),
            pl.BlockSpec((d, tn), lambda i, j: (0, j)),
            pl.BlockSpec((1, tn), lambda i, j: (0, j)),
        ],
        out_specs=pl.BlockSpec((tm, tn), lambda i, j: (i, j)),
        out_shape=jax.ShapeDtypeStruct((t, n), BF16),
        scratch_shapes=[pltpu.VMEM((tm, d), BF16)],
        compiler_params=_params("arbitrary", "arbitrary"),
        name="norm_in_proj",
    )(x, mod, g.reshape(1, d), w, b.reshape(1, n))


def _gelu_tanh(x):
    return 0.5 * x * (1.0 + jnp.tanh(0.7978845608028654 * (x + 0.044715 * (x * x * x))))


def _softplus(z):
    return jnp.maximum(z, 0.0) + jnp.log1p(jnp.exp(-jnp.abs(z)))


def _scan_columns(a_ref, b_ref, h0, n_rows, cols):
    width = cols.stop - cols.start
    row = lax.broadcasted_iota(jnp.int32, (SUBLANES, width), 0)

    def body(m, carry):
        r = pl.ds(pl.multiple_of(m * SUBLANES, SUBLANES), SUBLANES)
        a = a_ref[r, cols]
        b = b_ref[r, cols]
        for d in (1, 2, 4):
            keep = row >= d
            a_prev = pltpu.roll(a, d, 0)
            b_prev = pltpu.roll(b, d, 0)
            b = jnp.where(keep, a * b_prev, 0.0) + b
            a = jnp.where(keep, a * a_prev, a)
        h = a * carry + b
        b_ref[r, cols] = h
        return h[SUBLANES - 1:SUBLANES, :]

    return lax.fori_loop(0, n_rows // SUBLANES, body, h0, unroll=2)


def _lru_kernel(u_ref, gate_ref, mg_ref, convw_ref, convb_ref, wg_ref, bg_ref, lam_ref,
                wout_ref, o_ref, ubuf, a_s, b_s, y_s, hcarry, *, scan_cols):
    ts = u_ref.shape[0]
    halo = SUBLANES
    first = pl.program_id(1) == 0

    @pl.when(first)
    def _():
        ubuf[0:halo, :] = jnp.zeros((halo, D_MODEL), F32)
        hcarry[...] = jnp.zeros_like(hcarry)

    @pl.when(jnp.logical_not(first))
    def _():
        ubuf[0:halo, :] = ubuf[ts:ts + halo, :]

    ubuf[halo:halo + ts, :] = u_ref[...].astype(F32)

    neg_c_softplus = -LRU_C * _softplus(-lam_ref[...])

    for hd in range(LRU_HEADS):
        cs = slice(hd * LRU_HEAD_DIM, (hd + 1) * LRU_HEAD_DIM)
        uc = convb_ref[:, cs]
        for k in range(CONV_W):
            off = halo - (CONV_W - 1) + k
            uc = uc + convw_ref[k:k + 1, cs] * ubuf[off:off + ts, cs]
        g = jnp.dot(uc.astype(BF16), wg_ref[hd], preferred_element_type=F32) + bg_ref[hd]
        r = jax.nn.sigmoid(g[:, :LRU_HEAD_DIM])
        i = jax.nn.sigmoid(g[:, LRU_HEAD_DIM:])
        log_a = r * neg_c_softplus[:, cs]
        a = jnp.exp(log_a)
        a_s[:, cs] = a
        b_s[:, cs] = jnp.sqrt((1.0 + a * a) * jnp.tanh(-log_a)) * (i * uc)

    for c0 in range(0, D_MODEL, scan_cols):
        cols = slice(c0, c0 + scan_cols)
        hcarry[:, cols] = _scan_columns(a_s, b_s, hcarry[:, cols], ts, cols)

    for hd in range(LRU_HEADS):
        cs = slice(hd * LRU_HEAD_DIM, (hd + 1) * LRU_HEAD_DIM)
        y_s[:, cs] = (b_s[:, cs] * _gelu_tanh(gate_ref[:, cs].astype(F32))).astype(BF16)

    out = jnp.dot(y_s[...], wout_ref[...], preferred_element_type=F32)
    o_ref[...] = (jax.nn.sigmoid(mg_ref[...].astype(F32)) * out).astype(o_ref.dtype)


def _lru_branch(proj, conv_w, conv_b, wg, bg, lam, w_out, batch, seq, ts):
    t = proj.shape[0]
    d = D_MODEL
    per_seq = seq // ts
    row = lambda b, s: b * per_seq + s
    return pl.pallas_call(
        functools.partial(_lru_kernel, scan_cols=512),
        grid=(batch, per_seq),
        in_specs=[
            pl.BlockSpec((ts, d), lambda b, s: (row(b, s), COL_U)),
            pl.BlockSpec((ts, d), lambda b, s: (row(b, s), COL_GATE)),
            pl.BlockSpec((ts, d), lambda b, s: (row(b, s), COL_GA)),
            _resident((CONV_W, d), lambda b, s: (0, 0)),
            _resident((1, d), lambda b, s: (0, 0)),
            _resident((LRU_HEADS, LRU_HEAD_DIM, 2 * LRU_HEAD_DIM), lambda b, s: (0, 0, 0)),
            _resident((LRU_HEADS, 1, 2 * LRU_HEAD_DIM), lambda b, s: (0, 0, 0)),
            _resident((1, d), lambda b, s: (0, 0)),
            _resident((d, d), lambda b, s: (0, 0)),
        ],
        out_specs=pl.BlockSpec((ts, d), lambda b, s: (row(b, s), 0)),
        out_shape=jax.ShapeDtypeStruct((t, d), BF16),
        scratch_shapes=[
            pltpu.VMEM((ts + SUBLANES, d), F32),
            pltpu.VMEM((ts, d), F32),
            pltpu.VMEM((ts, d), F32),
            pltpu.VMEM((ts, d), BF16),
            pltpu.VMEM((1, d), F32),
        ],
        compiler_params=_params("arbitrary", "arbitrary"),
        name="rglru_branch",
    )(proj, proj, proj, conv_w, conv_b.reshape(1, d), wg, bg, lam.reshape(1, d), w_out)


def _attn_kernel(sinks_ref, q_ref, k_ref, v_ref, kp_ref, vp_ref, ma_ref, gb_ref, x_ref,
                 mod_ref, wa_ref, wo_ref, o_ref, y_s):
    tq = q_ref.shape[0]
    n_blk = tq // WINDOW
    first = pl.program_id(1) == 0
    rows = GQA_GROUP * WINDOW

    qi = lax.broadcasted_iota(jnp.int32, (rows, WINDOW), 0) % WINDOW
    kk = lax.broadcasted_iota(jnp.int32, (rows, WINDOW), 1)
    in_cur = kk <= qi
    scale = HEAD_DIM ** -0.5
    nt_dims = (((1,), (1,)), ((), ()))

    for h in range(N_KV_HEADS):
        hs = slice(h * HEAD_DIM, (h + 1) * HEAD_DIM)
        sink = jnp.concatenate(
            [jnp.full((WINDOW, 1), sinks_ref[h * GQA_GROUP + g], F32) for g in range(GQA_GROUP)],
            axis=0)
        for n in range(n_blk):
            rs = slice(n * WINDOW, (n + 1) * WINDOW)
            q4 = jnp.concatenate(
                [q_ref[rs, (h * GQA_GROUP + g) * HEAD_DIM:(h * GQA_GROUP + g + 1) * HEAD_DIM]
                 for g in range(GQA_GROUP)], axis=0)
            if n == 0:
                k_prev, v_prev = kp_ref[:, hs], vp_ref[:, hs]
            else:
                ps = slice((n - 1) * WINDOW, n * WINDOW)
                k_prev, v_prev = k_ref[ps, hs], v_ref[ps, hs]
            k_cat = jnp.concatenate([k_prev, k_ref[rs, hs]], axis=0)
            v_cat = jnp.concatenate([v_prev, v_ref[rs, hs]], axis=0)
            s = lax.dot_general(q4, k_cat, nt_dims, preferred_element_type=F32)
            s_prev = s[:, :WINDOW]
            if n == 0:
                s_prev = jnp.where(first, -jnp.inf, s_prev)
            sc = jnp.where(in_cur, s[:, WINDOW:], s_prev) * scale
            m = jnp.maximum(jnp.max(sc, axis=-1, keepdims=True), sink)
            p = jnp.exp(sc - m)
            denom = jnp.sum(p, axis=-1, keepdims=True) + jnp.exp(sink - m)
            p_cat = jnp.concatenate(
                [jnp.where(in_cur, 0.0, p), jnp.where(in_cur, p, 0.0)], axis=1).astype(BF16)
            o = jnp.dot(p_cat, v_cat, preferred_element_type=F32) / denom
            for g in range(GQA_GROUP):
                cs = slice((h * GQA_GROUP + g) * HEAD_DIM, (h * GQA_GROUP + g + 1) * HEAD_DIM)
                y_s[rs, cs] = o[g * WINDOW:(g + 1) * WINDOW].astype(BF16)

    attn = jnp.dot(y_s[...], wa_ref[...], preferred_element_type=F32)
    merged = jax.nn.sigmoid(gb_ref[...].astype(F32)) * attn + ma_ref[...].astype(F32)
    r = jnp.dot(merged.astype(BF16), wo_ref[...], preferred_element_type=F32)
    o_ref[...] = x_ref[...] + mod_ref[0, 2:3, :] * r


def _attn_merge(proj, merged_a, x, sinks, mod, w_attn, w_o, batch, seq, tq):
    t, d = x.shape
    per_seq = seq // tq
    n_blk = tq // WINDOW
    row = lambda b, s: b * per_seq + s
    prev_blk = lambda b, s: jnp.maximum(row(b, s) * n_blk - 1, 0)
    return pl.pallas_call(
        _attn_kernel,
        grid=(batch, per_seq),
        in_specs=[
            pl.BlockSpec(memory_space=pltpu.SMEM),
            pl.BlockSpec((tq, d), lambda b, s: (row(b, s), COL_Q)),
            pl.BlockSpec((tq, KV_WIDTH), lambda b, s: (row(b, s), COL_K)),
            pl.BlockSpec((tq, KV_WIDTH), lambda b, s: (row(b, s), COL_V)),
            pl.BlockSpec((WINDOW, KV_WIDTH), lambda b, s: (prev_blk(b, s), COL_K)),
            pl.BlockSpec((WINDOW, KV_WIDTH), lambda b, s: (prev_blk(b, s), COL_V)),
            pl.BlockSpec((tq, d), lambda b, s: (row(b, s), 0)),
            pl.BlockSpec((tq, d), lambda b, s: (row(b, s), COL_GB)),
            pl.BlockSpec((tq, d), lambda b, s: (row(b, s), 0)),
            pl.BlockSpec((1, N_MOD, d), lambda b, s: (b, 0, 0)),
            _resident((d, d), lambda b, s: (0, 0)),
            _resident((d, d), lambda b, s: (0, 0)),
        ],
        out_specs=pl.BlockSpec((tq, d), lambda b, s: (row(b, s), 0)),
        out_shape=jax.ShapeDtypeStruct((t, d), F32),
        scratch_shapes=[pltpu.VMEM((tq, d), BF16)],
        compiler_params=_params("arbitrary", "arbitrary"),
        name="swa_merge_out_proj",
    )(sinks, proj, proj, proj, proj, proj, merged_a, proj, x, mod, w_attn, w_o)


def _ffn_kernel(x_ref, mod_ref, g_ref, wg_ref, wu_ref, wo_ref, fg_ref, o_ref, h_ref, acc_ref,
                *, rows, final_norm):
    tm = x_ref.shape[0]
    j = pl.program_id(1)

    @pl.when(j == 0)
    def _():
        def body(i, carry):
            r = pl.ds(pl.multiple_of(i * rows, rows), rows)
            h = _norm_mod_rows(x_ref[r, :], g_ref[...], mod_ref[0, 3:4, :], mod_ref[0, 4:5, :])
            h_ref[r, :] = h.astype(BF16)
            return carry
        lax.fori_loop(0, tm // rows, body, 0)

    h = h_ref[...]
    gate = jnp.dot(h, wg_ref[...], preferred_element_type=F32)
    up = jnp.dot(h, wu_ref[...], preferred_element_type=F32)
    act = (gate * jax.nn.sigmoid(gate) * up).astype(BF16)
    contrib = jnp.dot(act, wo_ref[...], preferred_element_type=F32)

    @pl.when(j == 0)
    def _():
        acc_ref[...] = contrib

    @pl.when(j > 0)
    def _():
        acc_ref[...] += contrib

    @pl.when(j == pl.num_programs(1) - 1)
    def _():
        def body(i, carry):
            r = pl.ds(pl.multiple_of(i * rows, rows), rows)
            y = x_ref[r, :] + mod_ref[0, 5:6, :] * acc_ref[r, :]
            if final_norm:
                inv = lax.rsqrt(jnp.mean(y * y, axis=-1, keepdims=True) + EPS)
                y = (y * inv) * fg_ref[...]
            o_ref[r, :] = y
            return carry
        lax.fori_loop(0, tm // rows, body, 0)


def _ffn(x, mod, g, w_in, w_out, final_g, final_norm, seq, tm, th):
    t, d = x.shape
    hidden = w_out.shape[0]
    per_seq = seq // tm
    n_h = hidden // th
    return pl.pallas_call(
        functools.partial(_ffn_kernel, rows=min(tm, 128), final_norm=final_norm),
        grid=(t // tm, n_h),
        in_specs=[
            pl.BlockSpec((tm, d), lambda i, j: (i, 0)),
            pl.BlockSpec((1, N_MOD, d), lambda i, j: (i // per_seq, 0, 0)),
            pl.BlockSpec((1, d), lambda i, j: (0, 0)),
            pl.BlockSpec((d, th), lambda i, j: (0, j)),
            pl.BlockSpec((d, th), lambda i, j: (0, j + n_h)),
            pl.BlockSpec((th, d), lambda i, j: (j, 0)),
            pl.BlockSpec((1, d), lambda i, j: (0, 0)),
        ],
        out_specs=pl.BlockSpec((tm, d), lambda i, j: (i, 0)),
        out_shape=jax.ShapeDtypeStruct((t, d), F32),
        scratch_shapes=[pltpu.VMEM((tm, d), BF16), pltpu.VMEM((tm, d), F32)],
        compiler_params=_params("arbitrary", "arbitrary"),
        name="norm_swiglu",
    )(x, mod, g.reshape(1, d), w_in, w_in, w_out, final_g.reshape(1, d))


def _reorder_in_proj(a):
    d = D_MODEL
    kv = a[..., 3 * d:3 * d + 2 * KV_WIDTH]
    return jnp.concatenate([a[..., :3 * d], a[..., 3 * d + 2 * KV_WIDTH:], kv], axis=-1)


def kernel(x, c, ada_w, ada_b, norm1_g, w_in, b_in, conv_w, conv_b, lru_wa, lru_ba, lru_wx,
           lru_bx, lru_lambda, sinks, w_lru_out, w_attn_out, w_o, norm2_g, w_ffn_in, w_ffn_out,
           final_g):
    batch, seq, d = x.shape
    assert d == D_MODEL and seq % WINDOW == 0
    depth = ada_w.shape[0]
    t = batch * seq
    tm_proj = min(seq, 1024)
    tm_ffn = min(seq, 512)
    ts_lru = min(seq, 256)
    tq_attn = min(seq, 256)

    mod = _modulation(c, ada_w, ada_b)
    xf = x.reshape(t, d)
    for l in range(depth):
        w_in_l = _reorder_in_proj(w_in[l]).astype(BF16)
        b_in_l = _reorder_in_proj(b_in[l])
        proj = _in_proj(xf, mod[l], norm1_g[l], w_in_l, b_in_l, seq, tm_proj, 512)

        wg = jnp.concatenate([lru_wa[l], lru_wx[l]], axis=-1).astype(BF16)
        bg = jnp.concatenate([lru_ba[l].reshape(LRU_HEADS, 1, LRU_HEAD_DIM),
                              lru_bx[l].reshape(LRU_HEADS, 1, LRU_HEAD_DIM)], axis=-1)
        merged_a = _lru_branch(proj, conv_w[l], conv_b[l], wg, bg, lru_lambda[l],
                               w_lru_out[l].astype(BF16), batch, seq, ts_lru)

        xf = _attn_merge(proj, merged_a, xf, sinks[l], mod[l], w_attn_out[l].astype(BF16),
                         w_o[l].astype(BF16), batch, seq, tq_attn)

        xf = _ffn(xf, mod[l], norm2_g[l], w_ffn_in[l].astype(BF16), w_ffn_out[l].astype(BF16),
                  final_g, l == depth - 1, seq, tm_ffn, 512)
    return xf.reshape(batch, seq, d)
```

```python
import functools

import jax
import jax.numpy as jnp
from jax import lax
from jax.experimental import pallas as pl
from jax.experimental.pallas import tpu as pltpu

F32 = jnp.float32
BF16 = jnp.bfloat16

D_MODEL = 2048
LRU_HEADS = 16
LRU_HEAD_DIM = D_MODEL // LRU_HEADS
CONV_W = 4
LRU_C = 8.0
HEAD_DIM = 128
N_Q_HEADS = 16
N_KV_HEADS = 4
GQA_GROUP = N_Q_HEADS // N_KV_HEADS
WINDOW = 128
KV_WIDTH = N_KV_HEADS * HEAD_DIM
N_MOD = 6
EPS = 1e-6
C_IN = 5 * D_MODEL + 2 * KV_WIDTH

PROJ_TN = KV_WIDTH
_BLK = D_MODEL // PROJ_TN
OUT_U, OUT_GATE, OUT_Q, OUT_GA, OUT_GB = 0, 1, 2, 3, 4
OUT_K = 5 * _BLK
OUT_V = OUT_K + 1
N_PROJ_BLK = C_IN // PROJ_TN

SUBLANES = 8
LANES = 128
VMEM_LIMIT = 56 * 1024 * 1024


def _src_col_block(j):
    return jnp.where(j < 3 * _BLK, j, jnp.where(j < 5 * _BLK, j + 2, j - 2 * _BLK))


def _params(*sem):
    return pltpu.CompilerParams(dimension_semantics=sem, vmem_limit_bytes=VMEM_LIMIT)


def _resident(shape, index_map):
    return pl.BlockSpec(shape, index_map, pipeline_mode=pl.Buffered(1))


def _norm_mod_rows(x, g, shift, scale):
    inv = lax.rsqrt(jnp.mean(x * x, axis=-1, keepdims=True) + EPS)
    return (x * inv) * g * (1.0 + scale) + shift


def _gelu_tanh(x):
    return 0.5 * x * (1.0 + jnp.tanh(0.7978845608028654 * (x + 0.044715 * (x * x * x))))


def _softplus(z):
    return jnp.maximum(z, 0.0) + jnp.log1p(jnp.exp(-jnp.abs(z)))


def _mod_kernel(c_ref, w_ref, b_ref, o_ref):
    c = c_ref[...]
    act = (c * jax.nn.sigmoid(c)).astype(BF16)
    o_ref[0] = jnp.dot(act, w_ref[0].astype(BF16), preferred_element_type=F32) + b_ref[0]


def _modulation(c, ada_w, ada_b, tn=1024):
    depth, d, n = ada_w.shape
    b = c.shape[0]
    rows = -(-b // SUBLANES) * SUBLANES
    c_pad = jnp.zeros((rows, d), F32).at[:b].set(c)
    out = pl.pallas_call(
        _mod_kernel,
        grid=(depth, n // tn),
        in_specs=[
            pl.BlockSpec((rows, d), lambda l, j: (0, 0)),
            pl.BlockSpec((1, d, tn), lambda l, j: (l, 0, j)),
            pl.BlockSpec((1, 1, tn), lambda l, j: (l, 0, j)),
        ],
        out_specs=pl.BlockSpec((1, rows, tn), lambda l, j: (l, 0, j)),
        out_shape=jax.ShapeDtypeStruct((depth, rows, n), F32),
        compiler_params=_params("arbitrary", "arbitrary"),
        name="adaln_modulation",
    )(c_pad, ada_w, ada_b.reshape(depth, 1, n))
    return out[:, :b].reshape(depth, b, N_MOD, d)


def _inproj_kernel(x_ref, mod_ref, g_ref, w_ref, b_ref, o_ref, h_ref, *, rows):
    tm = x_ref.shape[0]
    j = pl.program_id(1)

    @pl.when(j == 0)
    def _():
        def body(i, carry):
            r = pl.ds(pl.multiple_of(i * rows, rows), rows)
            h = _norm_mod_rows(x_ref[r, :], g_ref[...], mod_ref[0, 0:1, :], mod_ref[0, 1:2, :])
            h_ref[r, :] = h.astype(BF16)
            return carry
        lax.fori_loop(0, tm // rows, body, 0)

    def project(act):
        z = jnp.dot(h_ref[...], w_ref[...], preferred_element_type=F32) + b_ref[...]
        o_ref[...] = act(z).astype(o_ref.dtype)

    blk = j // _BLK
    is_gelu = blk == OUT_GATE
    is_sigmoid = jnp.logical_or(blk == OUT_GA, blk == OUT_GB)

    @pl.when(is_gelu)
    def _():
        project(_gelu_tanh)

    @pl.when(is_sigmoid)
    def _():
        project(jax.nn.sigmoid)

    @pl.when(jnp.logical_not(jnp.logical_or(is_gelu, is_sigmoid)))
    def _():
        project(lambda z: z)


def _in_proj(x, mod, g, w, b, layer, seq, tm):
    t, d = x.shape
    tn = PROJ_TN
    per_seq = seq // tm
    return pl.pallas_call(
        functools.partial(_inproj_kernel, rows=min(tm, 128)),
        grid=(t // tm, N_PROJ_BLK),
        in_specs=[
            pl.BlockSpec((tm, d), lambda i, j: (i, 0)),
            pl.BlockSpec((1, N_MOD, d), lambda i, j: (i // per_seq, 0, 0)),
            pl.BlockSpec((1, d), lambda i, j: (0, 0)),
            pl.BlockSpec((None, d, tn), lambda i, j: (layer, 0, _src_col_block(j))),
            pl.BlockSpec((None, 1, tn), lambda i, j: (layer, 0, _src_col_block(j))),
        ],
        out_specs=pl.BlockSpec((tm, tn), lambda i, j: (i, j)),
        out_shape=jax.ShapeDtypeStruct((t, C_IN), BF16),
        scratch_shapes=[pltpu.VMEM((tm, d), BF16)],
        compiler_params=_params("arbitrary", "arbitrary"),
        name="norm_in_proj",
    )(x, mod, g.reshape(1, d), w, b)


def _lru_kernel(u_ref, gate_ref, mg_ref, convw_ref, convb_ref, wg_ref, bg_ref, lam_ref,
                wout_ref, o_ref, ubuf, a_s, b_s, y_s, hcarry, *, group):
    ts = u_ref.shape[0]
    sub = ts // SUBLANES
    pitch = sub + 4
    halo = SUBLANES
    first = pl.program_id(1) == 0

    @pl.when(first)
    def _():
        ubuf[:, 0:halo, :] = jnp.zeros((LRU_HEADS, halo, LANES), F32)
        hcarry[...] = jnp.zeros_like(hcarry)

    @pl.when(jnp.logical_not(first))
    def _():
        ubuf[:, 0:halo, :] = ubuf[:, ts:ts + halo, :]

    neg_c_softplus = -LRU_C * _softplus(-lam_ref[...])

    for hd in range(LRU_HEADS):
        cs = slice(hd * LRU_HEAD_DIM, (hd + 1) * LRU_HEAD_DIM)
        ubuf[hd, halo:halo + ts, :] = u_ref[:, cs].astype(F32)
        uc = convb_ref[:, cs]
        for k in range(CONV_W):
            off = halo - (CONV_W - 1) + k
            uc = uc + convw_ref[k:k + 1, cs] * ubuf[hd, off:off + ts, :]
        g = jnp.dot(uc.astype(BF16), wg_ref[hd], preferred_element_type=F32) + bg_ref[hd]
        r = jax.nn.sigmoid(g[:, :LRU_HEAD_DIM])
        i = jax.nn.sigmoid(g[:, LRU_HEAD_DIM:])
        log_a = r * neg_c_softplus[:, cs]
        a = jnp.exp(log_a)
        b = jnp.sqrt((1.0 + a * a) * jnp.tanh(-log_a)) * (i * uc)
        for s in range(SUBLANES):
            a_s[hd, s * pitch:s * pitch + sub, :] = a[s * sub:(s + 1) * sub, :]
            b_s[hd, s * pitch:s * pitch + sub, :] = b[s * sub:(s + 1) * sub, :]

    row = lax.broadcasted_iota(jnp.int32, (SUBLANES, LANES), 0)
    zeros = jnp.zeros((SUBLANES, LANES), F32)
    ones = jnp.ones((SUBLANES, LANES), F32)

    for h0 in range(0, LRU_HEADS, group):
        heads = range(h0, h0 + group)

        def strided(ref, hd, m):
            return ref[hd, pl.ds(m, SUBLANES, stride=pitch), :]

        def pass1(m, carry):
            out = []
            for n, hd in enumerate(heads):
                a_m = strided(a_s, hd, m)
                out += [a_m * carry[2 * n] + strided(b_s, hd, m), carry[2 * n + 1] * a_m]
            return tuple(out)

        ends = lax.fori_loop(0, sub, pass1, (zeros, ones) * group, unroll=2)

        starts = []
        for n, hd in enumerate(heads):
            e, q = ends[2 * n], ends[2 * n + 1]
            for d in (1, 2, 4):
                keep = row >= d
                e_prev = pltpu.roll(e, d, 0)
                q_prev = pltpu.roll(q, d, 0)
                e = jnp.where(keep, q * e_prev, 0.0) + e
                q = jnp.where(keep, q * q_prev, q)
            cs = slice(hd * LRU_HEAD_DIM, (hd + 1) * LRU_HEAD_DIM)
            h_in = hcarry[:, cs]
            end_state = e + q * h_in
            hcarry[:, cs] = end_state[SUBLANES - 1:SUBLANES, :]
            starts.append(jnp.where(row == 0, h_in, pltpu.roll(end_state, 1, 0)))

        def pass2(m, carry):
            out = []
            for n, hd in enumerate(heads):
                h = strided(a_s, hd, m) * carry[n] + strided(b_s, hd, m)
                b_s[hd, pl.ds(m, SUBLANES, stride=pitch), :] = h
                out.append(h)
            return tuple(out)

        lax.fori_loop(0, sub, pass2, tuple(starts), unroll=2)

    for hd in range(LRU_HEADS):
        cs = slice(hd * LRU_HEAD_DIM, (hd + 1) * LRU_HEAD_DIM)
        h = jnp.concatenate(
            [b_s[hd, s * pitch:s * pitch + sub, :] for s in range(SUBLANES)], axis=0)
        y_s[:, cs] = (h * gate_ref[:, cs].astype(F32)).astype(BF16)

    out = jnp.dot(y_s[...], wout_ref[...], preferred_element_type=F32)
    o_ref[...] = (mg_ref[...].astype(F32) * out).astype(o_ref.dtype)


def _lru_branch(proj, conv_w, conv_b, wg, bg, lam, w_out, layer, batch, seq, ts):
    t = proj.shape[0]
    d = D_MODEL
    per_seq = seq // ts
    pitch = ts // SUBLANES + 4
    row = lambda b, s: b * per_seq + s
    return pl.pallas_call(
        functools.partial(_lru_kernel, group=8),
        grid=(batch, per_seq),
        in_specs=[
            pl.BlockSpec((ts, d), lambda b, s: (row(b, s), OUT_U)),
            pl.BlockSpec((ts, d), lambda b, s: (row(b, s), OUT_GATE)),
            pl.BlockSpec((ts, d), lambda b, s: (row(b, s), OUT_GA)),
            _resident((CONV_W, d), lambda b, s: (0, 0)),
            _resident((1, d), lambda b, s: (0, 0)),
            _resident((LRU_HEADS, LRU_HEAD_DIM, 2 * LRU_HEAD_DIM), lambda b, s: (0, 0, 0)),
            _resident((LRU_HEADS, 1, 2 * LRU_HEAD_DIM), lambda b, s: (0, 0, 0)),
            _resident((1, d), lambda b, s: (0, 0)),
            _resident((None, d, d), lambda b, s: (layer, 0, 0)),
        ],
        out_specs=pl.BlockSpec((ts, d), lambda b, s: (row(b, s), 0)),
        out_shape=jax.ShapeDtypeStruct((t, d), BF16),
        scratch_shapes=[
            pltpu.VMEM((LRU_HEADS, ts + SUBLANES, LANES), F32),
            pltpu.VMEM((LRU_HEADS, SUBLANES * pitch, LANES), F32),
            pltpu.VMEM((LRU_HEADS, SUBLANES * pitch, LANES), F32),
            pltpu.VMEM((ts, d), BF16),
            pltpu.VMEM((1, d), F32),
        ],
        compiler_params=_params("arbitrary", "arbitrary"),
        name="rglru_branch",
    )(proj, proj, proj, conv_w, conv_b.reshape(1, d), wg, bg, lam.reshape(1, d), w_out)


def _attn_kernel(sinks_ref, q_ref, k_ref, v_ref, kp_ref, vp_ref, ma_ref, gb_ref, x_ref,
                 mod_ref, wa_ref, wo_ref, o_ref, y_s):
    tq = q_ref.shape[0]
    n_blk = tq // WINDOW
    first = pl.program_id(1) == 0
    rows = GQA_GROUP * WINDOW

    qi = lax.broadcasted_iota(jnp.int32, (rows, WINDOW), 0) % WINDOW
    kk = lax.broadcasted_iota(jnp.int32, (rows, WINDOW), 1)
    in_cur = kk <= qi
    scale = HEAD_DIM ** -0.5
    nt_dims = (((1,), (1,)), ((), ()))

    for h in range(N_KV_HEADS):
        hs = slice(h * HEAD_DIM, (h + 1) * HEAD_DIM)
        sink = jnp.concatenate(
            [jnp.full((WINDOW, 1), sinks_ref[h * GQA_GROUP + g], F32) for g in range(GQA_GROUP)],
            axis=0)
        for n in range(n_blk):
            rs = slice(n * WINDOW, (n + 1) * WINDOW)
            q4 = jnp.concatenate(
                [q_ref[rs, (h * GQA_GROUP + g) * HEAD_DIM:(h * GQA_GROUP + g + 1) * HEAD_DIM]
                 for g in range(GQA_GROUP)], axis=0)
            if n == 0:
                k_prev, v_prev = kp_ref[:, hs], vp_ref[:, hs]
            else:
                ps = slice((n - 1) * WINDOW, n * WINDOW)
                k_prev, v_prev = k_ref[ps, hs], v_ref[ps, hs]
            k_cat = jnp.concatenate([k_prev, k_ref[rs, hs]], axis=0)
            v_cat = jnp.concatenate([v_prev, v_ref[rs, hs]], axis=0)
            s = lax.dot_general(q4, k_cat, nt_dims, preferred_element_type=F32)
            s_prev = s[:, :WINDOW]
            if n == 0:
                s_prev = jnp.where(first, -jnp.inf, s_prev)
            sc = jnp.where(in_cur, s[:, WINDOW:], s_prev) * scale
            m = jnp.maximum(jnp.max(sc, axis=-1, keepdims=True), sink)
            p = jnp.exp(sc - m)
            denom = jnp.sum(p, axis=-1, keepdims=True) + jnp.exp(sink - m)
            p_cat = jnp.concatenate(
                [jnp.where(in_cur, 0.0, p), jnp.where(in_cur, p, 0.0)], axis=1).astype(BF16)
            o = jnp.dot(p_cat, v_cat, preferred_element_type=F32) / denom
            for g in range(GQA_GROUP):
                cs = slice((h * GQA_GROUP + g) * HEAD_DIM, (h * GQA_GROUP + g + 1) * HEAD_DIM)
                y_s[rs, cs] = o[g * WINDOW:(g + 1) * WINDOW].astype(BF16)

    attn = jnp.dot(y_s[...], wa_ref[...], preferred_element_type=F32)
    merged = gb_ref[...].astype(F32) * attn + ma_ref[...].astype(F32)
    r = jnp.dot(merged.astype(BF16), wo_ref[...], preferred_element_type=F32)
    o_ref[...] = x_ref[...] + mod_ref[0, 2:3, :] * r


def _attn_merge(proj, merged_a, x, sinks, mod, w_attn, w_o, layer, batch, seq, tq):
    t, d = x.shape
    per_seq = seq // tq
    n_blk = tq // WINDOW
    row = lambda b, s: b * per_seq + s
    prev_blk = lambda b, s: jnp.maximum(row(b, s) * n_blk - 1, 0)
    return pl.pallas_call(
        _attn_kernel,
        grid=(batch, per_seq),
        in_specs=[
            pl.BlockSpec(memory_space=pltpu.SMEM),
            pl.BlockSpec((tq, d), lambda b, s: (row(b, s), OUT_Q)),
            pl.BlockSpec((tq, KV_WIDTH), lambda b, s: (row(b, s), OUT_K)),
            pl.BlockSpec((tq, KV_WIDTH), lambda b, s: (row(b, s), OUT_V)),
            pl.BlockSpec((WINDOW, KV_WIDTH), lambda b, s: (prev_blk(b, s), OUT_K)),
            pl.BlockSpec((WINDOW, KV_WIDTH), lambda b, s: (prev_blk(b, s), OUT_V)),
            pl.BlockSpec((tq, d), lambda b, s: (row(b, s), 0)),
            pl.BlockSpec((tq, d), lambda b, s: (row(b, s), OUT_GB)),
            pl.BlockSpec((tq, d), lambda b, s: (row(b, s), 0)),
            pl.BlockSpec((1, N_MOD, d), lambda b, s: (b, 0, 0)),
            _resident((None, d, d), lambda b, s: (layer, 0, 0)),
            _resident((None, d, d), lambda b, s: (layer, 0, 0)),
        ],
        out_specs=pl.BlockSpec((tq, d), lambda b, s: (row(b, s), 0)),
        out_shape=jax.ShapeDtypeStruct((t, d), F32),
        scratch_shapes=[pltpu.VMEM((tq, d), BF16)],
        compiler_params=_params("arbitrary", "arbitrary"),
        name="swa_merge_out_proj",
    )(sinks, proj, proj, proj, proj, proj, merged_a, proj, x, mod, w_attn, w_o)


def _ffn_kernel(x_ref, mod_ref, g_ref, wg_ref, wu_ref, wo_ref, fg_ref, o_ref, h_ref, acc_ref,
                *, rows, final_norm):
    tm, d = x_ref.shape
    th = wg_ref.shape[1]
    j = pl.program_id(1)

    @pl.when(j == 0)
    def _():
        def body(i, carry):
            r = pl.ds(pl.multiple_of(i * rows, rows), rows)
            h = _norm_mod_rows(x_ref[r, :], g_ref[...], mod_ref[0, 3:4, :], mod_ref[0, 4:5, :])
            h_ref[r, :] = h.astype(BF16)
            acc_ref[r, :] = jnp.zeros((rows, d), F32)
            return carry
        lax.fori_loop(0, tm // rows, body, 0)

    h = h_ref[...]
    acts = []
    for c0 in range(0, th, 256):
        gate = jnp.dot(h, wg_ref[:, c0:c0 + 256], preferred_element_type=F32)
        up = jnp.dot(h, wu_ref[:, c0:c0 + 256], preferred_element_type=F32)
        acts.append((gate * jax.nn.sigmoid(gate) * up).astype(BF16))
    act = jnp.concatenate(acts, axis=1)
    for c0 in range(0, d, 512):
        acc_ref[:, c0:c0 + 512] += jnp.dot(act, wo_ref[:, c0:c0 + 512],
                                            preferred_element_type=F32)

    @pl.when(j == pl.num_programs(1) - 1)
    def _():
        def body(i, carry):
            r = pl.ds(pl.multiple_of(i * rows, rows), rows)
            y = x_ref[r, :] + mod_ref[0, 5:6, :] * acc_ref[r, :]
            if final_norm:
                inv = lax.rsqrt(jnp.mean(y * y, axis=-1, keepdims=True) + EPS)
                y = (y * inv) * fg_ref[...]
            o_ref[r, :] = y
            return carry
        lax.fori_loop(0, tm // rows, body, 0)


def _ffn(x, mod, g, w_in, w_out, final_g, final_norm, layer, seq, tm, th):
    t, d = x.shape
    hidden = w_out.shape[1]
    per_seq = seq // tm
    n_h = hidden // th
    return pl.pallas_call(
        functools.partial(_ffn_kernel, rows=min(tm, 128), final_norm=final_norm),
        grid=(t // tm, n_h),
        in_specs=[
            pl.BlockSpec((tm, d), lambda i, j: (i, 0)),
            pl.BlockSpec((1, N_MOD, d), lambda i, j: (i // per_seq, 0, 0)),
            pl.BlockSpec((1, d), lambda i, j: (0, 0)),
            pl.BlockSpec((None, d, th), lambda i, j: (layer, 0, j)),
            pl.BlockSpec((None, d, th), lambda i, j: (layer, 0, j + n_h)),
            pl.BlockSpec((None, th, d), lambda i, j: (layer, j, 0)),
            pl.BlockSpec((1, d), lambda i, j: (0, 0)),
        ],
        out_specs=pl.BlockSpec((tm, d), lambda i, j: (i, 0)),
        out_shape=jax.ShapeDtypeStruct((t, d), F32),
        scratch_shapes=[pltpu.VMEM((tm, d), BF16), pltpu.VMEM((tm, d), F32)],
        compiler_params=_params("arbitrary", "arbitrary"),
        name="norm_swiglu",
    )(x, mod, g.reshape(1, d), w_in, w_in, w_out, final_g.reshape(1, d))


def kernel(x, c, ada_w, ada_b, norm1_g, w_in, b_in, conv_w, conv_b, lru_wa, lru_ba, lru_wx,
           lru_bx, lru_lambda, sinks, w_lru_out, w_attn_out, w_o, norm2_g, w_ffn_in, w_ffn_out,
           final_g):
    batch, seq, d = x.shape
    assert d == D_MODEL and seq % WINDOW == 0
    depth = ada_w.shape[0]
    t = batch * seq
    tm_proj = min(seq, 1024)
    tm_ffn = min(seq, 512)
    ts_lru = min(seq, 256)
    tq_attn = min(seq, 256)

    w_in_bf = w_in.astype(BF16)
    w_lru_bf = w_lru_out.astype(BF16)
    w_attn_bf = w_attn_out.astype(BF16)
    w_o_bf = w_o.astype(BF16)
    w_ffn_in_bf = w_ffn_in.astype(BF16)
    w_ffn_out_bf = w_ffn_out.astype(BF16)
    wg = jnp.concatenate([lru_wa, lru_wx], axis=-1).astype(BF16)
    bg = jnp.concatenate([lru_ba.reshape(depth, LRU_HEADS, 1, LRU_HEAD_DIM),
                          lru_bx.reshape(depth, LRU_HEADS, 1, LRU_HEAD_DIM)], axis=-1)
    b_in3 = b_in.reshape(depth, 1, C_IN)

    mod = _modulation(c, ada_w, ada_b)
    xf = x.reshape(t, d)
    for l in range(depth):
        proj = _in_proj(xf, mod[l], norm1_g[l], w_in_bf, b_in3, l, seq, tm_proj)
        merged_a = _lru_branch(proj, conv_w[l], conv_b[l], wg[l], bg[l], lru_lambda[l],
                               w_lru_bf, l, batch, seq, ts_lru)
        xf = _attn_merge(proj, merged_a, xf, sinks[l], mod[l], w_attn_bf, w_o_bf, l,
                         batch, seq, tq_attn)
        xf = _ffn(xf, mod[l], norm2_g[l], w_ffn_in_bf, w_ffn_out_bf, final_g,
                  l == depth - 1, l, seq, tm_ffn, 512)
    return xf.reshape(batch, seq, d)
```

```python
import functools

import jax
import jax.numpy as jnp
from jax import lax
from jax.experimental import pallas as pl
from jax.experimental.pallas import tpu as pltpu

F32 = jnp.float32
BF16 = jnp.bfloat16

D_MODEL = 2048
LRU_HEADS = 16
LRU_HEAD_DIM = D_MODEL // LRU_HEADS
CONV_W = 4
LRU_C = 8.0
HEAD_DIM = 128
N_Q_HEADS = 16
N_KV_HEADS = 4
GQA_GROUP = N_Q_HEADS // N_KV_HEADS
WINDOW = 128
KV_WIDTH = N_KV_HEADS * HEAD_DIM
N_MOD = 6
EPS = 1e-6
C_IN = 5 * D_MODEL + 2 * KV_WIDTH

PROJ_TN = 2 * KV_WIDTH
_BLK = D_MODEL // PROJ_TN
_KV_BLK = 2 * KV_WIDTH // PROJ_TN
OUT_U, OUT_GATE, OUT_Q, OUT_GA, OUT_GB = 0, 1, 2, 3, 4
OUT_K = 5 * D_MODEL // KV_WIDTH
OUT_V = OUT_K + 1
N_PROJ_BLK = C_IN // PROJ_TN

SUBLANES = 8
LANES = 128
VMEM_LIMIT = 56 * 1024 * 1024


def _src_col_block(j):
    return jnp.where(j < 3 * _BLK, j, jnp.where(j < 5 * _BLK, j + _KV_BLK, j - 2 * _BLK))


def _params(*sem):
    return pltpu.CompilerParams(dimension_semantics=sem, vmem_limit_bytes=VMEM_LIMIT)


def _resident(shape, index_map):
    return pl.BlockSpec(shape, index_map, pipeline_mode=pl.Buffered(1))


def _norm_mod_rows(x, g, shift, scale):
    inv = lax.rsqrt(jnp.mean(x * x, axis=-1, keepdims=True) + EPS)
    return (x * inv) * g * (1.0 + scale) + shift


def _gelu_tanh(x):
    return 0.5 * x * (1.0 + jnp.tanh(0.7978845608028654 * (x + 0.044715 * (x * x * x))))


def _softplus(z):
    return jnp.maximum(z, 0.0) + jnp.log1p(jnp.exp(-jnp.abs(z)))


def _mod_kernel(c_ref, w_ref, b_ref, o_ref):
    c = c_ref[...]
    act = (c * jax.nn.sigmoid(c)).astype(BF16)
    o_ref[0] = jnp.dot(act, w_ref[0].astype(BF16), preferred_element_type=F32) + b_ref[0]


def _modulation(c, ada_w, ada_b, tn=1024):
    depth, d, n = ada_w.shape
    b = c.shape[0]
    rows = -(-b // SUBLANES) * SUBLANES
    c_pad = jnp.zeros((rows, d), F32).at[:b].set(c)
    out = pl.pallas_call(
        _mod_kernel,
        grid=(depth, n // tn),
        in_specs=[
            pl.BlockSpec((rows, d), lambda l, j: (0, 0)),
            pl.BlockSpec((1, d, tn), lambda l, j: (l, 0, j)),
            pl.BlockSpec((1, 1, tn), lambda l, j: (l, 0, j)),
        ],
        out_specs=pl.BlockSpec((1, rows, tn), lambda l, j: (l, 0, j)),
        out_shape=jax.ShapeDtypeStruct((depth, rows, n), F32),
        compiler_params=_params("arbitrary", "arbitrary"),
        name="adaln_modulation",
    )(c_pad, ada_w, ada_b.reshape(depth, 1, n))
    return out[:, :b].reshape(depth, b, N_MOD, d)


def _inproj_kernel(x_ref, mod_ref, g_ref, w_ref, b_ref, o_ref, h_ref, *, rows, chunk):
    tm = x_ref.shape[0]
    j = pl.program_id(1)

    @pl.when(j == 0)
    def _():
        def body(i, carry):
            r = pl.ds(pl.multiple_of(i * rows, rows), rows)
            h = _norm_mod_rows(x_ref[r, :], g_ref[...], mod_ref[0, 0:1, :], mod_ref[0, 1:2, :])
            h_ref[r, :] = h.astype(BF16)
            return carry
        lax.fori_loop(0, tm // rows, body, 0)

    def project(act):
        for c0 in range(0, w_ref.shape[1], chunk):
            cols = slice(c0, c0 + chunk)
            z = jnp.dot(h_ref[...], w_ref[:, cols], preferred_element_type=F32) + b_ref[:, cols]
            o_ref[:, cols] = act(z).astype(o_ref.dtype)

    blk = j // _BLK
    is_gelu = blk == OUT_GATE
    is_sigmoid = jnp.logical_or(blk == OUT_GA, blk == OUT_GB)

    @pl.when(is_gelu)
    def _():
        project(_gelu_tanh)

    @pl.when(is_sigmoid)
    def _():
        project(jax.nn.sigmoid)

    @pl.when(jnp.logical_not(jnp.logical_or(is_gelu, is_sigmoid)))
    def _():
        project(lambda z: z)


def _in_proj(x, mod, g, w, b, layer, seq, tm):
    t, d = x.shape
    tn = PROJ_TN
    per_seq = seq // tm
    return pl.pallas_call(
        functools.partial(_inproj_kernel, rows=min(tm, 128), chunk=PROJ_TN),
        grid=(t // tm, N_PROJ_BLK),
        in_specs=[
            pl.BlockSpec((tm, d), lambda i, j: (i, 0)),
            pl.BlockSpec((1, N_MOD, d), lambda i, j: (i // per_seq, 0, 0)),
            pl.BlockSpec((1, d), lambda i, j: (0, 0)),
            pl.BlockSpec((None, d, tn), lambda i, j: (layer, 0, _src_col_block(j))),
            pl.BlockSpec((None, 1, tn), lambda i, j: (layer, 0, _src_col_block(j))),
        ],
        out_specs=pl.BlockSpec((tm, tn), lambda i, j: (i, j)),
        out_shape=jax.ShapeDtypeStruct((t, C_IN), BF16),
        scratch_shapes=[pltpu.VMEM((tm, d), BF16)],
        compiler_params=_params("arbitrary", "arbitrary"),
        name="norm_in_proj",
    )(x, mod, g.reshape(1, d), w, b)


def _lru_kernel(u_ref, gate_ref, mg_ref, convw_ref, convb_ref, wg_ref, bg_ref, lam_ref,
                wout_ref, o_ref, ubuf, a_s, b_s, y_s, hcarry, *, group, per_seq):
    ts = u_ref.shape[0]
    sub = ts // SUBLANES
    pitch = sub + 4
    halo = SUBLANES
    g_idx = pl.program_id(0)
    first = g_idx % per_seq == 0

    @pl.when(g_idx == 0)
    def _():
        ubuf[:, ts:ts + halo, :] = jnp.zeros((LRU_HEADS, halo, LANES), F32)
        hcarry[...] = jnp.zeros_like(hcarry)
        y_s[...] = jnp.zeros_like(y_s)

    ubuf[:, 0:halo, :] = jnp.where(first, 0.0, ubuf[:, ts:ts + halo, :])
    hcarry[...] = jnp.where(first, 0.0, hcarry[...])

    neg_c_softplus = -LRU_C * _softplus(-lam_ref[...])
    out_chunks = 4
    out_cols = D_MODEL // out_chunks
    heads_per_chunk = LRU_HEADS // out_chunks

    for hd in range(LRU_HEADS):
        if hd % heads_per_chunk == 0:
            c0 = (hd // heads_per_chunk) * out_cols
            out = jnp.dot(y_s[...], wout_ref[:, c0:c0 + out_cols], preferred_element_type=F32)
            o_ref[:, c0:c0 + out_cols] = (
                mg_ref[:, c0:c0 + out_cols].astype(F32) * out).astype(o_ref.dtype)
        cs = slice(hd * LRU_HEAD_DIM, (hd + 1) * LRU_HEAD_DIM)
        ubuf[hd, halo:halo + ts, :] = u_ref[:, cs].astype(F32)
        uc = convb_ref[:, cs]
        for k in range(CONV_W):
            off = halo - (CONV_W - 1) + k
            uc = uc + convw_ref[k:k + 1, cs] * ubuf[hd, off:off + ts, :]
        g = jnp.dot(uc.astype(BF16), wg_ref[hd], preferred_element_type=F32) + bg_ref[hd]
        r = jax.nn.sigmoid(g[:, :LRU_HEAD_DIM])
        i = jax.nn.sigmoid(g[:, LRU_HEAD_DIM:])
        log_a = r * neg_c_softplus[:, cs]
        a = jnp.exp(log_a)
        b = jnp.sqrt((1.0 + a * a) * jnp.tanh(-log_a)) * (i * uc)
        for s in range(SUBLANES):
            a_s[hd, s * pitch:s * pitch + sub, :] = a[s * sub:(s + 1) * sub, :]
            b_s[hd, s * pitch:s * pitch + sub, :] = b[s * sub:(s + 1) * sub, :]

    row = lax.broadcasted_iota(jnp.int32, (SUBLANES, LANES), 0)
    zeros = jnp.zeros((SUBLANES, LANES), F32)
    ones = jnp.ones((SUBLANES, LANES), F32)

    for h0 in range(0, LRU_HEADS, group):
        heads = range(h0, h0 + group)

        def strided(ref, hd, m):
            return ref[hd, pl.ds(m, SUBLANES, stride=pitch), :]

        def pass1(m, carry):
            out = []
            for n, hd in enumerate(heads):
                a_m = strided(a_s, hd, m)
                out += [a_m * carry[2 * n] + strided(b_s, hd, m), carry[2 * n + 1] * a_m]
            return tuple(out)

        ends = lax.fori_loop(0, sub, pass1, (zeros, ones) * group, unroll=2)

        starts = []
        for n, hd in enumerate(heads):
            e, q = ends[2 * n], ends[2 * n + 1]
            for d in (1, 2, 4):
                keep = row >= d
                e_prev = pltpu.roll(e, d, 0)
                q_prev = pltpu.roll(q, d, 0)
                e = jnp.where(keep, q * e_prev, 0.0) + e
                q = jnp.where(keep, q * q_prev, q)
            cs = slice(hd * LRU_HEAD_DIM, (hd + 1) * LRU_HEAD_DIM)
            h_in = hcarry[:, cs]
            end_state = e + q * h_in
            hcarry[:, cs] = end_state[SUBLANES - 1:SUBLANES, :]
            starts.append(jnp.where(row == 0, h_in, pltpu.roll(end_state, 1, 0)))

        def pass2(m, carry):
            out = []
            for n, hd in enumerate(heads):
                h = strided(a_s, hd, m) * carry[n] + strided(b_s, hd, m)
                b_s[hd, pl.ds(m, SUBLANES, stride=pitch), :] = h
                out.append(h)
            return tuple(out)

        lax.fori_loop(0, sub, pass2, tuple(starts), unroll=2)

    for hd in range(LRU_HEADS):
        cs = slice(hd * LRU_HEAD_DIM, (hd + 1) * LRU_HEAD_DIM)
        h = jnp.concatenate(
            [b_s[hd, s * pitch:s * pitch + sub, :] for s in range(SUBLANES)], axis=0)
        y_s[:, cs] = (h * gate_ref[:, cs].astype(F32)).astype(BF16)


def _lru_branch(proj, conv_w, conv_b, wg, bg, lam, w_out, layer, batch, seq, ts):
    t = proj.shape[0]
    d = D_MODEL
    per_seq = seq // ts
    n_tiles = batch * per_seq
    pitch = ts // SUBLANES + 4
    cur = lambda g: jnp.minimum(g, n_tiles - 1)
    prev = lambda g: jnp.maximum(g - 1, 0)
    return pl.pallas_call(
        functools.partial(_lru_kernel, group=8, per_seq=per_seq),
        grid=(n_tiles + 1,),
        in_specs=[
            pl.BlockSpec((ts, d), lambda g: (cur(g), OUT_U)),
            pl.BlockSpec((ts, d), lambda g: (cur(g), OUT_GATE)),
            pl.BlockSpec((ts, d), lambda g: (prev(g), OUT_GA)),
            _resident((CONV_W, d), lambda g: (0, 0)),
            _resident((1, d), lambda g: (0, 0)),
            _resident((LRU_HEADS, LRU_HEAD_DIM, 2 * LRU_HEAD_DIM), lambda g: (0, 0, 0)),
            _resident((LRU_HEADS, 1, 2 * LRU_HEAD_DIM), lambda g: (0, 0, 0)),
            _resident((1, d), lambda g: (0, 0)),
            _resident((None, d, d), lambda g: (layer, 0, 0)),
        ],
        out_specs=pl.BlockSpec((ts, d), lambda g: (prev(g), 0)),
        out_shape=jax.ShapeDtypeStruct((t, d), BF16),
        scratch_shapes=[
            pltpu.VMEM((LRU_HEADS, ts + SUBLANES, LANES), F32),
            pltpu.VMEM((LRU_HEADS, SUBLANES * pitch, LANES), F32),
            pltpu.VMEM((LRU_HEADS, SUBLANES * pitch, LANES), F32),
            pltpu.VMEM((ts, d), BF16),
            pltpu.VMEM((1, d), F32),
        ],
        compiler_params=_params("arbitrary"),
        name="rglru_branch",
    )(proj, proj, proj, conv_w, conv_b.reshape(1, d), wg, bg, lam.reshape(1, d), w_out)


def _attn_kernel(sinks_ref, q_ref, k_ref, v_ref, kp_ref, vp_ref, ma_ref, gb_ref, x_ref,
                 mod_ref, wa_ref, wo_ref, o_ref, y_s):
    tq = q_ref.shape[0]
    n_blk = tq // WINDOW
    first = pl.program_id(1) == 0
    rows = GQA_GROUP * WINDOW

    qi = lax.broadcasted_iota(jnp.int32, (rows, WINDOW), 0) % WINDOW
    kk = lax.broadcasted_iota(jnp.int32, (rows, WINDOW), 1)
    in_cur = kk <= qi
    scale = HEAD_DIM ** -0.5
    nt_dims = (((1,), (1,)), ((), ()))
    ones_blk = jnp.ones((2 * WINDOW, HEAD_DIM), BF16)

    for h in range(N_KV_HEADS):
        hs = slice(h * HEAD_DIM, (h + 1) * HEAD_DIM)
        sink = jnp.concatenate(
            [jnp.full((WINDOW, 1), sinks_ref[h * GQA_GROUP + g], F32) for g in range(GQA_GROUP)],
            axis=0)
        for n in range(n_blk):
            rs = slice(n * WINDOW, (n + 1) * WINDOW)
            q4 = jnp.concatenate(
                [q_ref[rs, (h * GQA_GROUP + g) * HEAD_DIM:(h * GQA_GROUP + g + 1) * HEAD_DIM]
                 for g in range(GQA_GROUP)], axis=0)
            if n == 0:
                k_prev, v_prev = kp_ref[:, hs], vp_ref[:, hs]
            else:
                ps = slice((n - 1) * WINDOW, n * WINDOW)
                k_prev, v_prev = k_ref[ps, hs], v_ref[ps, hs]
            k_cat = jnp.concatenate([k_prev, k_ref[rs, hs]], axis=0)
            v_cat = jnp.concatenate([v_prev, v_ref[rs, hs]], axis=0)
            s = lax.dot_general(q4, k_cat, nt_dims, preferred_element_type=F32)
            s_prev = s[:, :WINDOW]
            if n == 0:
                s_prev = jnp.where(first, -jnp.inf, s_prev)
            sc = jnp.where(in_cur, s[:, WINDOW:], s_prev) * scale
            m = jnp.maximum(jnp.max(sc, axis=-1, keepdims=True), sink)
            p = jnp.exp(sc - m)
            p_cat = jnp.concatenate(
                [jnp.where(in_cur, 0.0, p), jnp.where(in_cur, p, 0.0)], axis=1).astype(BF16)
            o_ext = jnp.dot(p_cat, jnp.concatenate([v_cat, ones_blk], axis=1),
                            preferred_element_type=F32)
            denom = o_ext[:, HEAD_DIM:] + jnp.exp(sink - m)
            o = o_ext[:, :HEAD_DIM] / denom
            for g in range(GQA_GROUP):
                cs = slice((h * GQA_GROUP + g) * HEAD_DIM, (h * GQA_GROUP + g + 1) * HEAD_DIM)
                y_s[rs, cs] = o[g * WINDOW:(g + 1) * WINDOW].astype(BF16)

    attn = jnp.dot(y_s[...], wa_ref[...], preferred_element_type=F32)
    merged = gb_ref[...].astype(F32) * attn + ma_ref[...].astype(F32)
    r = jnp.dot(merged.astype(BF16), wo_ref[...], preferred_element_type=F32)
    o_ref[...] = x_ref[...] + mod_ref[0, 2:3, :] * r


def _attn_merge(proj, merged_a, x, sinks, mod, w_attn, w_o, layer, batch, seq, tq):
    t, d = x.shape
    per_seq = seq // tq
    n_blk = tq // WINDOW
    row = lambda b, s: b * per_seq + s
    prev_blk = lambda b, s: jnp.maximum(row(b, s) * n_blk - 1, 0)
    return pl.pallas_call(
        _attn_kernel,
        grid=(batch, per_seq),
        in_specs=[
            pl.BlockSpec(memory_space=pltpu.SMEM),
            pl.BlockSpec((tq, d), lambda b, s: (row(b, s), OUT_Q)),
            pl.BlockSpec((tq, KV_WIDTH), lambda b, s: (row(b, s), OUT_K)),
            pl.BlockSpec((tq, KV_WIDTH), lambda b, s: (row(b, s), OUT_V)),
            pl.BlockSpec((WINDOW, KV_WIDTH), lambda b, s: (prev_blk(b, s), OUT_K)),
            pl.BlockSpec((WINDOW, KV_WIDTH), lambda b, s: (prev_blk(b, s), OUT_V)),
            pl.BlockSpec((tq, d), lambda b, s: (row(b, s), 0)),
            pl.BlockSpec((tq, d), lambda b, s: (row(b, s), OUT_GB)),
            pl.BlockSpec((tq, d), lambda b, s: (row(b, s), 0)),
            pl.BlockSpec((1, N_MOD, d), lambda b, s: (b, 0, 0)),
            _resident((None, d, d), lambda b, s: (layer, 0, 0)),
            _resident((None, d, d), lambda b, s: (layer, 0, 0)),
        ],
        out_specs=pl.BlockSpec((tq, d), lambda b, s: (row(b, s), 0)),
        out_shape=jax.ShapeDtypeStruct((t, d), F32),
        scratch_shapes=[pltpu.VMEM((tq, d), BF16)],
        compiler_params=_params("arbitrary", "arbitrary"),
        name="swa_merge_out_proj",
    )(sinks, proj, proj, proj, proj, proj, merged_a, proj, x, mod, w_attn, w_o)


def _ffn_kernel(x_ref, mod_ref, g_ref, wg_ref, wu_ref, wo_ref, fg_ref, o_ref, h_ref,
                *, rows, final_norm):
    acc_ref = o_ref
    tm, d = x_ref.shape
    th = wg_ref.shape[1]
    j = pl.program_id(1)

    @pl.when(j == 0)
    def _():
        def body(i, carry):
            r = pl.ds(pl.multiple_of(i * rows, rows), rows)
            h = _norm_mod_rows(x_ref[r, :], g_ref[...], mod_ref[0, 3:4, :], mod_ref[0, 4:5, :])
            h_ref[r, :] = h.astype(BF16)
            acc_ref[r, :] = jnp.zeros((rows, d), F32)
            return carry
        lax.fori_loop(0, tm // rows, body, 0)

    h = h_ref[...]
    acts = []
    for c0 in range(0, th, 256):
        gate = jnp.dot(h, wg_ref[:, c0:c0 + 256], preferred_element_type=F32)
        up = jnp.dot(h, wu_ref[:, c0:c0 + 256], preferred_element_type=F32)
        acts.append((gate * jax.nn.sigmoid(gate) * up).astype(BF16))
    act = jnp.concatenate(acts, axis=1)
    for c0 in range(0, d, 512):
        acc_ref[:, c0:c0 + 512] += jnp.dot(act, wo_ref[:, c0:c0 + 512],
                                            preferred_element_type=F32)

    @pl.when(j == pl.num_programs(1) - 1)
    def _():
        def body(i, carry):
            r = pl.ds(pl.multiple_of(i * rows, rows), rows)
            y = x_ref[r, :] + mod_ref[0, 5:6, :] * acc_ref[r, :]
            if final_norm:
                inv = lax.rsqrt(jnp.mean(y * y, axis=-1, keepdims=True) + EPS)
                y = (y * inv) * fg_ref[...]
            o_ref[r, :] = y
            return carry
        lax.fori_loop(0, tm // rows, body, 0)


def _ffn(x, mod, g, w_in, w_out, final_g, final_norm, layer, seq, tm, th):
    t, d = x.shape
    hidden = w_out.shape[1]
    per_seq = seq // tm
    n_h = hidden // th
    return pl.pallas_call(
        functools.partial(_ffn_kernel, rows=min(tm, 128), final_norm=final_norm),
        grid=(t // tm, n_h),
        in_specs=[
            pl.BlockSpec((tm, d), lambda i, j: (i, 0)),
            pl.BlockSpec((1, N_MOD, d), lambda i, j: (i // per_seq, 0, 0)),
            pl.BlockSpec((1, d), lambda i, j: (0, 0)),
            pl.BlockSpec((None, d, th), lambda i, j: (layer, 0, j)),
            pl.BlockSpec((None, d, th), lambda i, j: (layer, 0, j + n_h)),
            pl.BlockSpec((None, th, d), lambda i, j: (layer, j, 0)),
            pl.BlockSpec((1, d), lambda i, j: (0, 0)),
        ],
        out_specs=pl.BlockSpec((tm, d), lambda i, j: (i, 0)),
        out_shape=jax.ShapeDtypeStruct((t, d), F32),
        scratch_shapes=[pltpu.VMEM((tm, d), BF16)],
        compiler_params=_params("arbitrary", "arbitrary"),
        name="norm_swiglu",
    )(x, mod, g.reshape(1, d), w_in, w_in, w_out, final_g.reshape(1, d))


def kernel(x, c, ada_w, ada_b, norm1_g, w_in, b_in, conv_w, conv_b, lru_wa, lru_ba, lru_wx,
           lru_bx, lru_lambda, sinks, w_lru_out, w_attn_out, w_o, norm2_g, w_ffn_in, w_ffn_out,
           final_g):
    batch, seq, d = x.shape
    assert d == D_MODEL and seq % WINDOW == 0
    depth = ada_w.shape[0]
    t = batch * seq
    tm_proj = min(seq, 1024)
    tm_ffn = min(seq, 1024)
    ts_lru = min(seq, 256)
    tq_attn = min(seq, 256)

    w_in_bf = w_in.astype(BF16)
    w_lru_bf = w_lru_out.astype(BF16)
    w_attn_bf = w_attn_out.astype(BF16)
    w_o_bf = w_o.astype(BF16)
    w_ffn_in_bf = w_ffn_in.astype(BF16)
    w_ffn_out_bf = w_ffn_out.astype(BF16)
    wg = jnp.concatenate([lru_wa, lru_wx], axis=-1).astype(BF16)
    bg = jnp.concatenate([lru_ba.reshape(depth, LRU_HEADS, 1, LRU_HEAD_DIM),
                          lru_bx.reshape(depth, LRU_HEADS, 1, LRU_HEAD_DIM)], axis=-1)
    b_in3 = b_in.reshape(depth, 1, C_IN)

    mod = _modulation(c, ada_w, ada_b)
    xf = x.reshape(t, d)
    for l in range(depth):
        proj = _in_proj(xf, mod[l], norm1_g[l], w_in_bf, b_in3, l, seq, tm_proj)
        merged_a = _lru_branch(proj, conv_w[l], conv_b[l], wg[l], bg[l], lru_lambda[l],
                               w_lru_bf, l, batch, seq, ts_lru)
        xf = _attn_merge(proj, merged_a, xf, sinks[l], mod[l], w_attn_bf, w_o_bf, l,
                         batch, seq, tq_attn)
        xf = _ffn(xf, mod[l], norm2_g[l], w_ffn_in_bf, w_ffn_out_bf, final_g,
                  l == depth - 1, l, seq, tm_ffn, 512)
    return xf.reshape(batch, seq, d)
```

```python
import functools

import jax
import jax.numpy as jnp
from jax import lax
from jax.experimental import pallas as pl
from jax.experimental.pallas import tpu as pltpu

F32 = jnp.float32
BF16 = jnp.bfloat16

D_MODEL = 2048
LRU_HEADS = 16
LRU_HEAD_DIM = D_MODEL // LRU_HEADS
CONV_W = 4
LRU_C = 8.0
HEAD_DIM = 128
N_Q_HEADS = 16
N_KV_HEADS = 4
GQA_GROUP = N_Q_HEADS // N_KV_HEADS
WINDOW = 128
KV_WIDTH = N_KV_HEADS * HEAD_DIM
N_MOD = 6
EPS = 1e-6
C_IN = 5 * D_MODEL + 2 * KV_WIDTH

PROJ_TN = 2 * KV_WIDTH
_BLK = D_MODEL // PROJ_TN
_KV_BLK = 2 * KV_WIDTH // PROJ_TN
OUT_U, OUT_GATE, OUT_Q, OUT_GA, OUT_GB = 0, 1, 2, 3, 4
OUT_K = 5 * D_MODEL // KV_WIDTH
OUT_V = OUT_K + 1
N_PROJ_BLK = C_IN // PROJ_TN

SUBLANES = 8
LANES = 128
VMEM_LIMIT = 56 * 1024 * 1024


def _src_col_block(j):
    return jnp.where(j < 3 * _BLK, j, jnp.where(j < 5 * _BLK, j + _KV_BLK, j - 2 * _BLK))


def _params(*sem):
    return pltpu.CompilerParams(dimension_semantics=sem, vmem_limit_bytes=VMEM_LIMIT)


def _resident(shape, index_map):
    return pl.BlockSpec(shape, index_map, pipeline_mode=pl.Buffered(1))


def _norm_mod_into(x_ref, h_ref, gain, shift, rows, also_zero=None):
    tm, d = x_ref.shape
    group = 2 * SUBLANES

    def body(i, carry):
        for q in range(rows // group):
            r = pl.ds(pl.multiple_of(i * rows + q * group, group), group)
            x = x_ref[r, :]
            inv = lax.rsqrt(jnp.mean(x * x, axis=-1, keepdims=True) + EPS)
            h_ref[r, :] = ((x * inv) * gain + shift).astype(BF16)
            if also_zero is not None:
                also_zero[r, :] = jnp.zeros((group, d), F32)
        return carry

    lax.fori_loop(0, tm // rows, body, 0)


def _gelu_tanh(x):
    return 0.5 * x * (1.0 + jnp.tanh(0.7978845608028654 * (x + 0.044715 * (x * x * x))))


def _softplus(z):
    return jnp.maximum(z, 0.0) + jnp.log1p(jnp.exp(-jnp.abs(z)))


def _mod_kernel(c_ref, w_ref, b_ref, o_ref):
    c = c_ref[...]
    act = (c * jax.nn.sigmoid(c)).astype(BF16)
    o_ref[0] = jnp.dot(act, w_ref[0].astype(BF16), preferred_element_type=F32) + b_ref[0]


def _modulation(c, ada_w, ada_b, tn=1024):
    depth, d, n = ada_w.shape
    b = c.shape[0]
    rows = -(-b // SUBLANES) * SUBLANES
    c_pad = jnp.zeros((rows, d), F32).at[:b].set(c)
    out = pl.pallas_call(
        _mod_kernel,
        grid=(depth, n // tn),
        in_specs=[
            pl.BlockSpec((rows, d), lambda l, j: (0, 0)),
            pl.BlockSpec((1, d, tn), lambda l, j: (l, 0, j)),
            pl.BlockSpec((1, 1, tn), lambda l, j: (l, 0, j)),
        ],
        out_specs=pl.BlockSpec((1, rows, tn), lambda l, j: (l, 0, j)),
        out_shape=jax.ShapeDtypeStruct((depth, rows, n), F32),
        compiler_params=_params("arbitrary", "arbitrary"),
        name="adaln_modulation",
    )(c_pad, ada_w, ada_b.reshape(depth, 1, n))
    return out[:, :b].reshape(depth, b, N_MOD, d)


def _inproj_kernel(x_ref, mod_ref, g_ref, w_ref, b_ref, o_ref, h_ref, *, rows, chunk):
    tm = x_ref.shape[0]
    j = pl.program_id(1)

    @pl.when(j == 0)
    def _():
        _norm_mod_into(x_ref, h_ref, g_ref[...] * (1.0 + mod_ref[0, 1:2, :]), mod_ref[0, 0:1, :],
                       rows)

    def project(act):
        for c0 in range(0, w_ref.shape[1], chunk):
            cols = slice(c0, c0 + chunk)
            z = jnp.dot(h_ref[...], w_ref[:, cols].astype(BF16),
                        preferred_element_type=F32) + b_ref[:, cols]
            o_ref[:, cols] = act(z).astype(o_ref.dtype)

    blk = j // _BLK
    is_gelu = blk == OUT_GATE
    is_sigmoid = jnp.logical_or(blk == OUT_GA, blk == OUT_GB)

    @pl.when(is_gelu)
    def _():
        project(_gelu_tanh)

    @pl.when(is_sigmoid)
    def _():
        project(jax.nn.sigmoid)

    @pl.when(jnp.logical_not(jnp.logical_or(is_gelu, is_sigmoid)))
    def _():
        project(lambda z: z)


def _in_proj(x, mod, g, w, b, layer, seq, tm):
    t, d = x.shape
    tn = PROJ_TN
    per_seq = seq // tm
    return pl.pallas_call(
        functools.partial(_inproj_kernel, rows=min(tm, 128), chunk=PROJ_TN),
        grid=(t // tm, N_PROJ_BLK),
        in_specs=[
            pl.BlockSpec((tm, d), lambda i, j: (i, 0)),
            pl.BlockSpec((1, N_MOD, d), lambda i, j: (i // per_seq, 0, 0)),
            pl.BlockSpec((1, d), lambda i, j: (0, 0)),
            pl.BlockSpec((None, d, tn), lambda i, j: (layer, 0, _src_col_block(j))),
            pl.BlockSpec((None, 1, tn), lambda i, j: (layer, 0, _src_col_block(j))),
        ],
        out_specs=pl.BlockSpec((tm, tn), lambda i, j: (i, j)),
        out_shape=jax.ShapeDtypeStruct((t, C_IN), BF16),
        scratch_shapes=[pltpu.VMEM((tm, d), BF16)],
        compiler_params=_params("arbitrary", "arbitrary"),
        name="norm_in_proj",
    )(x, mod, g.reshape(1, d), w, b)


def _lru_kernel(u_ref, gate_ref, mg_ref, convw_ref, convb_ref, wg_ref, bg_ref, lam_ref,
                wout_ref, o_ref, ubuf, a_s, b_s, y_s, hcarry, wout_bf, *, group, per_seq):
    ts = u_ref.shape[0]
    sub = ts // SUBLANES
    pitch = sub + 4
    halo = SUBLANES
    g_idx = pl.program_id(0)
    first = g_idx % per_seq == 0

    @pl.when(g_idx == 0)
    def _():
        ubuf[:, ts:ts + halo, :] = jnp.zeros((LRU_HEADS, halo, LANES), F32)
        hcarry[...] = jnp.zeros_like(hcarry)
        y_s[...] = jnp.zeros_like(y_s)
        wout_bf[...] = wout_ref[...].astype(BF16)

    ubuf[:, 0:halo, :] = jnp.where(first, 0.0, ubuf[:, ts:ts + halo, :])
    hcarry[...] = jnp.where(first, 0.0, hcarry[...])

    neg_c_softplus = -LRU_C * _softplus(-lam_ref[...])
    out_chunks = 8
    out_cols = D_MODEL // out_chunks
    heads_per_chunk = LRU_HEADS // out_chunks

    for hd in range(LRU_HEADS):
        if hd % heads_per_chunk == 0:
            c0 = (hd // heads_per_chunk) * out_cols
            out = jnp.dot(y_s[...], wout_bf[:, c0:c0 + out_cols], preferred_element_type=F32)
            o_ref[:, c0:c0 + out_cols] = (
                mg_ref[:, c0:c0 + out_cols].astype(F32) * out).astype(o_ref.dtype)
        cs = slice(hd * LRU_HEAD_DIM, (hd + 1) * LRU_HEAD_DIM)
        ubuf[hd, halo:halo + ts, :] = u_ref[:, cs].astype(F32)
        uc = convb_ref[:, cs]
        for k in range(CONV_W):
            off = halo - (CONV_W - 1) + k
            uc = uc + convw_ref[k:k + 1, cs] * ubuf[hd, off:off + ts, :]
        g = jnp.dot(uc.astype(BF16), wg_ref[hd], preferred_element_type=F32) + bg_ref[hd]
        r = jax.nn.sigmoid(g[:, :LRU_HEAD_DIM])
        i = jax.nn.sigmoid(g[:, LRU_HEAD_DIM:])
        log_a = r * neg_c_softplus[:, cs]
        a = jnp.exp(log_a)
        b = jnp.sqrt((1.0 + a * a) * jnp.tanh(-log_a)) * (i * uc)
        for s in range(SUBLANES):
            a_s[hd, s * pitch:s * pitch + sub, :] = a[s * sub:(s + 1) * sub, :]
            b_s[hd, s * pitch:s * pitch + sub, :] = b[s * sub:(s + 1) * sub, :]

    row = lax.broadcasted_iota(jnp.int32, (SUBLANES, LANES), 0)
    zeros = jnp.zeros((SUBLANES, LANES), F32)
    ones = jnp.ones((SUBLANES, LANES), F32)

    for h0 in range(0, LRU_HEADS, group):
        heads = range(h0, h0 + group)

        def strided(ref, hd, m):
            return ref[hd, pl.ds(m, SUBLANES, stride=pitch), :]

        def pass1(m, carry):
            out = []
            for n, hd in enumerate(heads):
                a_m = strided(a_s, hd, m)
                out += [a_m * carry[2 * n] + strided(b_s, hd, m), carry[2 * n + 1] * a_m]
            return tuple(out)

        ends = lax.fori_loop(0, sub, pass1, (zeros, ones) * group, unroll=2)

        starts = []
        for n, hd in enumerate(heads):
            e, q = ends[2 * n], ends[2 * n + 1]
            for d in (1, 2, 4):
                keep = row >= d
                e_prev = pltpu.roll(e, d, 0)
                q_prev = pltpu.roll(q, d, 0)
                e = jnp.where(keep, q * e_prev, 0.0) + e
                q = jnp.where(keep, q * q_prev, q)
            cs = slice(hd * LRU_HEAD_DIM, (hd + 1) * LRU_HEAD_DIM)
            h_in = hcarry[:, cs]
            end_state = e + q * h_in
            hcarry[:, cs] = end_state[SUBLANES - 1:SUBLANES, :]
            starts.append(jnp.where(row == 0, h_in, pltpu.roll(end_state, 1, 0)))

        def pass2(m, carry):
            out = []
            for n, hd in enumerate(heads):
                h = strided(a_s, hd, m) * carry[n] + strided(b_s, hd, m)
                b_s[hd, pl.ds(m, SUBLANES, stride=pitch), :] = h
                out.append(h)
            return tuple(out)

        lax.fori_loop(0, sub, pass2, tuple(starts), unroll=2)

    for hd in range(LRU_HEADS):
        cs = slice(hd * LRU_HEAD_DIM, (hd + 1) * LRU_HEAD_DIM)
        h = jnp.concatenate(
            [b_s[hd, s * pitch:s * pitch + sub, :] for s in range(SUBLANES)], axis=0)
        y_s[:, cs] = (h * gate_ref[:, cs].astype(F32)).astype(BF16)


def _lru_branch(proj, conv_w, conv_b, wg, bg, lam, w_out, layer, batch, seq, ts):
    t = proj.shape[0]
    d = D_MODEL
    per_seq = seq // ts
    n_tiles = batch * per_seq
    pitch = ts // SUBLANES + 4
    cur = lambda g: jnp.minimum(g, n_tiles - 1)
    prev = lambda g: jnp.maximum(g - 1, 0)
    return pl.pallas_call(
        functools.partial(_lru_kernel, group=8, per_seq=per_seq),
        grid=(n_tiles + 1,),
        in_specs=[
            pl.BlockSpec((ts, d), lambda g: (cur(g), OUT_U)),
            pl.BlockSpec((ts, d), lambda g: (cur(g), OUT_GATE)),
            pl.BlockSpec((ts, d), lambda g: (prev(g), OUT_GA)),
            _resident((CONV_W, d), lambda g: (0, 0)),
            _resident((1, d), lambda g: (0, 0)),
            _resident((LRU_HEADS, LRU_HEAD_DIM, 2 * LRU_HEAD_DIM), lambda g: (0, 0, 0)),
            _resident((LRU_HEADS, 1, 2 * LRU_HEAD_DIM), lambda g: (0, 0, 0)),
            _resident((1, d), lambda g: (0, 0)),
            _resident((None, d, d), lambda g: (layer, 0, 0)),
        ],
        out_specs=pl.BlockSpec((ts, d), lambda g: (prev(g), 0)),
        out_shape=jax.ShapeDtypeStruct((t, d), BF16),
        scratch_shapes=[
            pltpu.VMEM((LRU_HEADS, ts + SUBLANES, LANES), F32),
            pltpu.VMEM((LRU_HEADS, SUBLANES * pitch, LANES), F32),
            pltpu.VMEM((LRU_HEADS, SUBLANES * pitch, LANES), F32),
            pltpu.VMEM((ts, d), BF16),
            pltpu.VMEM((1, d), F32),
            pltpu.VMEM((d, d), BF16),
        ],
        compiler_params=_params("arbitrary"),
        name="rglru_branch",
    )(proj, proj, proj, conv_w, conv_b.reshape(1, d), wg, bg, lam.reshape(1, d), w_out)


def _attn_kernel(sinks_ref, q_ref, k_ref, v_ref, kp_ref, vp_ref, ma_ref, gb_ref, x_ref,
                 mod_ref, wa_ref, wo_ref, o_ref, y_s):
    tq = q_ref.shape[0]
    n_blk = tq // WINDOW
    first = pl.program_id(1) == 0
    rows = GQA_GROUP * WINDOW

    qi = lax.broadcasted_iota(jnp.int32, (rows, WINDOW), 0) % WINDOW
    kk = lax.broadcasted_iota(jnp.int32, (rows, WINDOW), 1)
    in_cur = kk <= qi
    scale = HEAD_DIM ** -0.5
    nt_dims = (((1,), (1,)), ((), ()))
    ones_blk = jnp.ones((2 * WINDOW, HEAD_DIM), BF16)

    for h in range(N_KV_HEADS):
        hs = slice(h * HEAD_DIM, (h + 1) * HEAD_DIM)
        sink = jnp.concatenate(
            [jnp.full((WINDOW, 1), sinks_ref[h * GQA_GROUP + g], F32) for g in range(GQA_GROUP)],
            axis=0)
        for n in range(n_blk):
            rs = slice(n * WINDOW, (n + 1) * WINDOW)
            q4 = jnp.concatenate(
                [q_ref[rs, (h * GQA_GROUP + g) * HEAD_DIM:(h * GQA_GROUP + g + 1) * HEAD_DIM]
                 for g in range(GQA_GROUP)], axis=0)
            if n == 0:
                k_prev, v_prev = kp_ref[:, hs], vp_ref[:, hs]
            else:
                ps = slice((n - 1) * WINDOW, n * WINDOW)
                k_prev, v_prev = k_ref[ps, hs], v_ref[ps, hs]
            k_cat = jnp.concatenate([k_prev, k_ref[rs, hs]], axis=0)
            v_cat = jnp.concatenate([v_prev, v_ref[rs, hs]], axis=0)
            s = lax.dot_general(q4, k_cat, nt_dims, preferred_element_type=F32)
            s_prev = s[:, :WINDOW]
            if n == 0:
                s_prev = jnp.where(first, -jnp.inf, s_prev)
            sc = jnp.where(in_cur, s[:, WINDOW:], s_prev) * scale
            m = jnp.maximum(jnp.max(sc, axis=-1, keepdims=True), sink)
            p = jnp.exp(sc - m)
            p_cat = jnp.concatenate(
                [jnp.where(in_cur, 0.0, p), jnp.where(in_cur, p, 0.0)], axis=1).astype(BF16)
            o_ext = jnp.dot(p_cat, jnp.concatenate([v_cat, ones_blk], axis=1),
                            preferred_element_type=F32)
            denom = o_ext[:, HEAD_DIM:] + jnp.exp(sink - m)
            o = o_ext[:, :HEAD_DIM] / denom
            for g in range(GQA_GROUP):
                cs = slice((h * GQA_GROUP + g) * HEAD_DIM, (h * GQA_GROUP + g + 1) * HEAD_DIM)
                y_s[rs, cs] = o[g * WINDOW:(g + 1) * WINDOW].astype(BF16)

    attn = jnp.dot(y_s[...], wa_ref[...], preferred_element_type=F32)
    merged = gb_ref[...].astype(F32) * attn + ma_ref[...].astype(F32)
    r = jnp.dot(merged.astype(BF16), wo_ref[...], preferred_element_type=F32)
    o_ref[...] = x_ref[...] + mod_ref[0, 2:3, :] * r


def _attn_merge(proj, merged_a, x, sinks, mod, w_attn, w_o, layer, batch, seq, tq):
    t, d = x.shape
    per_seq = seq // tq
    n_blk = tq // WINDOW
    row = lambda b, s: b * per_seq + s
    prev_blk = lambda b, s: jnp.maximum(row(b, s) * n_blk - 1, 0)
    return pl.pallas_call(
        _attn_kernel,
        grid=(batch, per_seq),
        in_specs=[
            pl.BlockSpec(memory_space=pltpu.SMEM),
            pl.BlockSpec((tq, d), lambda b, s: (row(b, s), OUT_Q)),
            pl.BlockSpec((tq, KV_WIDTH), lambda b, s: (row(b, s), OUT_K)),
            pl.BlockSpec((tq, KV_WIDTH), lambda b, s: (row(b, s), OUT_V)),
            pl.BlockSpec((WINDOW, KV_WIDTH), lambda b, s: (prev_blk(b, s), OUT_K)),
            pl.BlockSpec((WINDOW, KV_WIDTH), lambda b, s: (prev_blk(b, s), OUT_V)),
            pl.BlockSpec((tq, d), lambda b, s: (row(b, s), 0)),
            pl.BlockSpec((tq, d), lambda b, s: (row(b, s), OUT_GB)),
            pl.BlockSpec((tq, d), lambda b, s: (row(b, s), 0)),
            pl.BlockSpec((1, N_MOD, d), lambda b, s: (b, 0, 0)),
            _resident((None, d, d), lambda b, s: (layer, 0, 0)),
            _resident((None, d, d), lambda b, s: (layer, 0, 0)),
        ],
        out_specs=pl.BlockSpec((tq, d), lambda b, s: (row(b, s), 0)),
        out_shape=jax.ShapeDtypeStruct((t, d), F32),
        scratch_shapes=[pltpu.VMEM((tq, d), BF16)],
        compiler_params=_params("arbitrary", "arbitrary"),
        name="swa_merge_out_proj",
    )(sinks, proj, proj, proj, proj, proj, merged_a, proj, x, mod, w_attn, w_o)


def _ffn_kernel(x_ref, mod_ref, g_ref, wg_ref, wu_ref, wo_ref, fg_ref, o_ref, h_ref,
                *, rows, final_norm):
    acc_ref = o_ref
    tm, d = x_ref.shape
    th = wg_ref.shape[1]
    j = pl.program_id(1)

    @pl.when(j == 0)
    def _():
        _norm_mod_into(x_ref, h_ref, g_ref[...] * (1.0 + mod_ref[0, 4:5, :]), mod_ref[0, 3:4, :],
                       rows, also_zero=acc_ref)

    h = h_ref[...]
    acts = []
    for c0 in range(0, th, 256):
        gate = jnp.dot(h, wg_ref[:, c0:c0 + 256], preferred_element_type=F32)
        up = jnp.dot(h, wu_ref[:, c0:c0 + 256], preferred_element_type=F32)
        acts.append((gate * jax.nn.sigmoid(gate) * up).astype(BF16))
    act = jnp.concatenate(acts, axis=1)
    for c0 in range(0, d, 512):
        acc_ref[:, c0:c0 + 512] += jnp.dot(act, wo_ref[:, c0:c0 + 512],
                                            preferred_element_type=F32)

    @pl.when(j == pl.num_programs(1) - 1)
    def _():
        def body(i, carry):
            r = pl.ds(pl.multiple_of(i * rows, rows), rows)
            y = x_ref[r, :] + mod_ref[0, 5:6, :] * acc_ref[r, :]
            if final_norm:
                inv = lax.rsqrt(jnp.mean(y * y, axis=-1, keepdims=True) + EPS)
                y = (y * inv) * fg_ref[...]
            o_ref[r, :] = y
            return carry
        lax.fori_loop(0, tm // rows, body, 0)


def _ffn(x, mod, g, w_in, w_out, final_g, final_norm, layer, seq, tm, th):
    t, d = x.shape
    hidden = w_out.shape[1]
    per_seq = seq // tm
    n_h = hidden // th
    return pl.pallas_call(
        functools.partial(_ffn_kernel, rows=min(tm, 128), final_norm=final_norm),
        grid=(t // tm, n_h),
        in_specs=[
            pl.BlockSpec((tm, d), lambda i, j: (i, 0)),
            pl.BlockSpec((1, N_MOD, d), lambda i, j: (i // per_seq, 0, 0)),
            pl.BlockSpec((1, d), lambda i, j: (0, 0)),
            pl.BlockSpec((None, d, th), lambda i, j: (layer, 0, j)),
            pl.BlockSpec((None, d, th), lambda i, j: (layer, 0, j + n_h)),
            pl.BlockSpec((None, th, d), lambda i, j: (layer, j, 0)),
            pl.BlockSpec((1, d), lambda i, j: (0, 0)),
        ],
        out_specs=pl.BlockSpec((tm, d), lambda i, j: (i, 0)),
        out_shape=jax.ShapeDtypeStruct((t, d), F32),
        scratch_shapes=[pltpu.VMEM((tm, d), BF16)],
        compiler_params=_params("arbitrary", "arbitrary"),
        name="norm_swiglu",
    )(x, mod, g.reshape(1, d), w_in, w_in, w_out, final_g.reshape(1, d))


def kernel(x, c, ada_w, ada_b, norm1_g, w_in, b_in, conv_w, conv_b, lru_wa, lru_ba, lru_wx,
           lru_bx, lru_lambda, sinks, w_lru_out, w_attn_out, w_o, norm2_g, w_ffn_in, w_ffn_out,
           final_g):
    batch, seq, d = x.shape
    assert d == D_MODEL and seq % WINDOW == 0
    depth = ada_w.shape[0]
    t = batch * seq
    tm_proj = min(seq, 1024)
    tm_ffn = min(seq, 1024)
    ts_lru = min(seq, 256)
    tq_attn = min(seq, 256)

    w_attn_bf = w_attn_out.astype(BF16)
    w_o_bf = w_o.astype(BF16)
    w_ffn_in_bf = w_ffn_in.astype(BF16)
    w_ffn_out_bf = w_ffn_out.astype(BF16)
    wg = jnp.concatenate([lru_wa, lru_wx], axis=-1).astype(BF16)
    bg = jnp.concatenate([lru_ba.reshape(depth, LRU_HEADS, 1, LRU_HEAD_DIM),
                          lru_bx.reshape(depth, LRU_HEADS, 1, LRU_HEAD_DIM)], axis=-1)
    b_in3 = b_in.reshape(depth, 1, C_IN)

    mod = _modulation(c, ada_w, ada_b)
    xf = x.reshape(t, d)
    for l in range(depth):
        proj = _in_proj(xf, mod[l], norm1_g[l], w_in, b_in3, l, seq, tm_proj)
        merged_a = _lru_branch(proj, conv_w[l], conv_b[l], wg[l], bg[l], lru_lambda[l],
                               w_lru_out, l, batch, seq, ts_lru)
        xf = _attn_merge(proj, merged_a, xf, sinks[l], mod[l], w_attn_bf, w_o_bf, l,
                         batch, seq, tq_attn)
        xf = _ffn(xf, mod[l], norm2_g[l], w_ffn_in_bf, w_ffn_out_bf, final_g,
                  l == depth - 1, l, seq, tm_ffn, 512)
    return xf.reshape(batch, seq, d)
```

```python
import functools

import jax
import jax.numpy as jnp
from jax import lax
from jax.experimental import pallas as pl
from jax.experimental.pallas import tpu as pltpu

F32 = jnp.float32
BF16 = jnp.bfloat16

D_MODEL = 2048
LRU_HEADS = 16
LRU_HEAD_DIM = D_MODEL // LRU_HEADS
CONV_W = 4
LRU_C = 8.0
HEAD_DIM = 128
N_Q_HEADS = 16
N_KV_HEADS = 4
GQA_GROUP = N_Q_HEADS // N_KV_HEADS
WINDOW = 128
KV_WIDTH = N_KV_HEADS * HEAD_DIM
N_MOD = 6
EPS = 1e-6
C_IN = 5 * D_MODEL + 2 * KV_WIDTH

PROJ_TN = 2 * KV_WIDTH
_BLK = D_MODEL // PROJ_TN
_KV_BLK = 2 * KV_WIDTH // PROJ_TN
OUT_U, OUT_GATE, OUT_Q, OUT_GA, OUT_GB = 0, 1, 2, 3, 4
OUT_K = 5 * D_MODEL // KV_WIDTH
OUT_V = OUT_K + 1
N_PROJ_BLK = C_IN // PROJ_TN

SUBLANES = 8
LANES = 128
VMEM_LIMIT = 56 * 1024 * 1024


def _src_col_block(j):
    return jnp.where(j < 3 * _BLK, j, jnp.where(j < 5 * _BLK, j + _KV_BLK, j - 2 * _BLK))


def _params(*sem):
    return pltpu.CompilerParams(dimension_semantics=sem, vmem_limit_bytes=VMEM_LIMIT)


def _cast_plan(rows, n_steps):
    k = n_steps
    while rows % k or (rows // k) % (2 * SUBLANES):
        k -= 1
    return rows // k, k


def _cast_specs(shape, layer, step_of, n_steps):
    rows, cols = shape[1:]
    blk, n_blk = _cast_plan(rows, n_steps)
    walk = lambda *g: jnp.minimum(step_of(*g), n_blk - 1)
    return (pl.BlockSpec((None, blk, cols), lambda *g: (layer, walk(*g), 0)),
            pl.BlockSpec((blk, cols), lambda *g: (walk(*g), 0)),
            jax.ShapeDtypeStruct((rows, cols), BF16))


def _resident(shape, index_map):
    return pl.BlockSpec(shape, index_map, pipeline_mode=pl.Buffered(1))


def _norm_mod_into(x_ref, h_ref, gain, shift, rows, also_zero=None):
    tm, d = x_ref.shape
    group = 2 * SUBLANES

    def body(i, carry):
        for q in range(rows // group):
            r = pl.ds(pl.multiple_of(i * rows + q * group, group), group)
            x = x_ref[r, :]
            inv = lax.rsqrt(jnp.mean(x * x, axis=-1, keepdims=True) + EPS)
            h_ref[r, :] = ((x * inv) * gain + shift).astype(BF16)
            if also_zero is not None:
                also_zero[r, :] = jnp.zeros((group, d), F32)
        return carry

    lax.fori_loop(0, tm // rows, body, 0)


def _gelu_tanh(x):
    return 0.5 * x * (1.0 + jnp.tanh(0.7978845608028654 * (x + 0.044715 * (x * x * x))))


def _softplus(z):
    return jnp.maximum(z, 0.0) + jnp.log1p(jnp.exp(-jnp.abs(z)))


def _mod_kernel(c_ref, w_ref, b_ref, o_ref):
    c = c_ref[...]
    act = (c * jax.nn.sigmoid(c)).astype(BF16)
    o_ref[0] = jnp.dot(act, w_ref[0].astype(BF16), preferred_element_type=F32) + b_ref[0]


def _modulation(c, ada_w, ada_b, tn=1024):
    depth, d, n = ada_w.shape
    b = c.shape[0]
    rows = -(-b // SUBLANES) * SUBLANES
    c_pad = jnp.zeros((rows, d), F32).at[:b].set(c)
    out = pl.pallas_call(
        _mod_kernel,
        grid=(depth, n // tn),
        in_specs=[
            pl.BlockSpec((rows, d), lambda l, j: (0, 0)),
            pl.BlockSpec((1, d, tn), lambda l, j: (l, 0, j)),
            pl.BlockSpec((1, 1, tn), lambda l, j: (l, 0, j)),
        ],
        out_specs=pl.BlockSpec((1, rows, tn), lambda l, j: (l, 0, j)),
        out_shape=jax.ShapeDtypeStruct((depth, rows, n), F32),
        compiler_params=_params("arbitrary", "arbitrary"),
        name="adaln_modulation",
    )(c_pad, ada_w, ada_b.reshape(depth, 1, n))
    return out[:, :b].reshape(depth, b, N_MOD, d)


def _inproj_kernel(x_ref, mod_ref, g_ref, w_ref, b_ref, o_ref, h_ref, *, rows, chunk):
    tm = x_ref.shape[0]
    j = pl.program_id(1)

    @pl.when(j == 0)
    def _():
        _norm_mod_into(x_ref, h_ref, g_ref[...] * (1.0 + mod_ref[0, 1:2, :]), mod_ref[0, 0:1, :],
                       rows)

    def project(act):
        for c0 in range(0, w_ref.shape[1], chunk):
            cols = slice(c0, c0 + chunk)
            z = jnp.dot(h_ref[...], w_ref[:, cols].astype(BF16),
                        preferred_element_type=F32) + b_ref[:, cols]
            o_ref[:, cols] = act(z).astype(o_ref.dtype)

    blk = j // _BLK
    is_gelu = blk == OUT_GATE
    is_sigmoid = jnp.logical_or(blk == OUT_GA, blk == OUT_GB)

    @pl.when(is_gelu)
    def _():
        project(_gelu_tanh)

    @pl.when(is_sigmoid)
    def _():
        project(jax.nn.sigmoid)

    @pl.when(jnp.logical_not(jnp.logical_or(is_gelu, is_sigmoid)))
    def _():
        project(lambda z: z)


def _in_proj(x, mod, g, w, b, layer, seq, tm):
    t, d = x.shape
    tn = PROJ_TN
    per_seq = seq // tm
    return pl.pallas_call(
        functools.partial(_inproj_kernel, rows=min(tm, 128), chunk=PROJ_TN),
        grid=(t // tm, N_PROJ_BLK),
        in_specs=[
            pl.BlockSpec((tm, d), lambda i, j: (i, 0)),
            pl.BlockSpec((1, N_MOD, d), lambda i, j: (i // per_seq, 0, 0)),
            pl.BlockSpec((1, d), lambda i, j: (0, 0)),
            pl.BlockSpec((None, d, tn), lambda i, j: (layer, 0, _src_col_block(j))),
            pl.BlockSpec((None, 1, tn), lambda i, j: (layer, 0, _src_col_block(j))),
        ],
        out_specs=pl.BlockSpec((tm, tn), lambda i, j: (i, j)),
        out_shape=jax.ShapeDtypeStruct((t, C_IN), BF16),
        scratch_shapes=[pltpu.VMEM((tm, d), BF16)],
        compiler_params=_params("arbitrary", "arbitrary"),
        name="norm_in_proj",
    )(x, mod, g.reshape(1, d), w, b)


def _lru_kernel(u_ref, gate_ref, mg_ref, convw_ref, convb_ref, wg_ref, bg_ref, lam_ref,
                wout_ref, wa32_ref, wo32_ref, o_ref, wa16_ref, wo16_ref,
                ubuf, a_s, b_s, y_s, hcarry, wout_bf, *, group, per_seq):
    wa16_ref[...] = wa32_ref[...].astype(BF16)
    wo16_ref[...] = wo32_ref[...].astype(BF16)
    ts = u_ref.shape[0]
    sub = ts // SUBLANES
    pitch = sub + 4
    halo = SUBLANES
    g_idx = pl.program_id(0)
    first = g_idx % per_seq == 0

    @pl.when(g_idx == 0)
    def _():
        ubuf[:, ts:ts + halo, :] = jnp.zeros((LRU_HEADS, halo, LANES), F32)
        hcarry[...] = jnp.zeros_like(hcarry)
        y_s[...] = jnp.zeros_like(y_s)
        wout_bf[...] = wout_ref[...].astype(BF16)

    ubuf[:, 0:halo, :] = jnp.where(first, 0.0, ubuf[:, ts:ts + halo, :])
    hcarry[...] = jnp.where(first, 0.0, hcarry[...])

    neg_c_softplus = -LRU_C * _softplus(-lam_ref[...])
    out_chunks = 8
    out_cols = D_MODEL // out_chunks
    heads_per_chunk = LRU_HEADS // out_chunks

    for hd in range(LRU_HEADS):
        if hd % heads_per_chunk == 0:
            c0 = (hd // heads_per_chunk) * out_cols
            out = jnp.dot(y_s[...], wout_bf[:, c0:c0 + out_cols], preferred_element_type=F32)
            o_ref[:, c0:c0 + out_cols] = (
                mg_ref[:, c0:c0 + out_cols].astype(F32) * out).astype(o_ref.dtype)
        cs = slice(hd * LRU_HEAD_DIM, (hd + 1) * LRU_HEAD_DIM)
        ubuf[hd, halo:halo + ts, :] = u_ref[:, cs].astype(F32)
        uc = convb_ref[:, cs]
        for k in range(CONV_W):
            off = halo - (CONV_W - 1) + k
            uc = uc + convw_ref[k:k + 1, cs] * ubuf[hd, off:off + ts, :]
        g = jnp.dot(uc.astype(BF16), wg_ref[hd], preferred_element_type=F32) + bg_ref[hd]
        r = jax.nn.sigmoid(g[:, :LRU_HEAD_DIM])
        i = jax.nn.sigmoid(g[:, LRU_HEAD_DIM:])
        log_a = r * neg_c_softplus[:, cs]
        a = jnp.exp(log_a)
        one_m_a2 = (1.0 + a * a) * jnp.tanh(-log_a)
        beta = jnp.where(one_m_a2 > 0.0, one_m_a2 * lax.rsqrt(one_m_a2), 0.0)
        b = beta * (i * uc)
        for s in range(SUBLANES):
            a_s[hd, s * pitch:s * pitch + sub, :] = a[s * sub:(s + 1) * sub, :]
            b_s[hd, s * pitch:s * pitch + sub, :] = b[s * sub:(s + 1) * sub, :]

    row = lax.broadcasted_iota(jnp.int32, (SUBLANES, LANES), 0)
    zeros = jnp.zeros((SUBLANES, LANES), F32)
    ones = jnp.ones((SUBLANES, LANES), F32)

    for h0 in range(0, LRU_HEADS, group):
        heads = range(h0, h0 + group)

        def strided(ref, hd, m):
            return ref[hd, pl.ds(m, SUBLANES, stride=pitch), :]

        def pass1(m, carry):
            out = []
            for n, hd in enumerate(heads):
                a_m = strided(a_s, hd, m)
                out += [a_m * carry[2 * n] + strided(b_s, hd, m), carry[2 * n + 1] * a_m]
            return tuple(out)

        ends = lax.fori_loop(0, sub, pass1, (zeros, ones) * group, unroll=2)

        starts = []
        for n, hd in enumerate(heads):
            e, q = ends[2 * n], ends[2 * n + 1]
            for d in (1, 2, 4):
                keep = row >= d
                e_prev = pltpu.roll(e, d, 0)
                q_prev = pltpu.roll(q, d, 0)
                e = jnp.where(keep, q * e_prev, 0.0) + e
                q = jnp.where(keep, q * q_prev, q)
            cs = slice(hd * LRU_HEAD_DIM, (hd + 1) * LRU_HEAD_DIM)
            h_in = hcarry[:, cs]
            end_state = e + q * h_in
            hcarry[:, cs] = end_state[SUBLANES - 1:SUBLANES, :]
            starts.append(jnp.where(row == 0, h_in, pltpu.roll(end_state, 1, 0)))

        def pass2(m, carry):
            out = []
            for n, hd in enumerate(heads):
                h = strided(a_s, hd, m) * carry[n] + strided(b_s, hd, m)
                b_s[hd, pl.ds(m, SUBLANES, stride=pitch), :] = h
                out.append(h)
            return tuple(out)

        lax.fori_loop(0, sub, pass2, tuple(starts), unroll=2)

    for hd in range(LRU_HEADS):
        cs = slice(hd * LRU_HEAD_DIM, (hd + 1) * LRU_HEAD_DIM)
        h = jnp.concatenate(
            [b_s[hd, s * pitch:s * pitch + sub, :] for s in range(SUBLANES)], axis=0)
        y_s[:, cs] = (h * gate_ref[:, cs].astype(F32)).astype(BF16)


def _lru_branch(proj, conv_w, conv_b, wg, bg, lam, w_out, w_attn, w_o, layer, batch, seq, ts):
    t = proj.shape[0]
    d = D_MODEL
    per_seq = seq // ts
    n_tiles = batch * per_seq
    pitch = ts // SUBLANES + 4
    cur = lambda g: jnp.minimum(g, n_tiles - 1)
    prev = lambda g: jnp.maximum(g - 1, 0)
    wa_in, wa_out, wa_shape = _cast_specs(w_attn.shape, layer, lambda g: g, n_tiles + 1)
    wo_in, wo_out, wo_shape = _cast_specs(w_o.shape, layer, lambda g: g, n_tiles + 1)
    return pl.pallas_call(
        functools.partial(_lru_kernel, group=8, per_seq=per_seq),
        grid=(n_tiles + 1,),
        in_specs=[
            pl.BlockSpec((ts, d), lambda g: (cur(g), OUT_U)),
            pl.BlockSpec((ts, d), lambda g: (cur(g), OUT_GATE)),
            pl.BlockSpec((ts, d), lambda g: (prev(g), OUT_GA)),
            _resident((CONV_W, d), lambda g: (0, 0)),
            _resident((1, d), lambda g: (0, 0)),
            _resident((LRU_HEADS, LRU_HEAD_DIM, 2 * LRU_HEAD_DIM), lambda g: (0, 0, 0)),
            _resident((LRU_HEADS, 1, 2 * LRU_HEAD_DIM), lambda g: (0, 0, 0)),
            _resident((1, d), lambda g: (0, 0)),
            _resident((None, d, d), lambda g: (layer, 0, 0)),
            wa_in,
            wo_in,
        ],
        out_specs=[pl.BlockSpec((ts, d), lambda g: (prev(g), 0)), wa_out, wo_out],
        out_shape=[jax.ShapeDtypeStruct((t, d), BF16), wa_shape, wo_shape],
        scratch_shapes=[
            pltpu.VMEM((LRU_HEADS, ts + SUBLANES, LANES), F32),
            pltpu.VMEM((LRU_HEADS, SUBLANES * pitch, LANES), F32),
            pltpu.VMEM((LRU_HEADS, SUBLANES * pitch, LANES), F32),
            pltpu.VMEM((ts, d), BF16),
            pltpu.VMEM((1, d), F32),
            pltpu.VMEM((d, d), BF16),
        ],
        compiler_params=_params("arbitrary"),
        name="rglru_branch",
    )(proj, proj, proj, conv_w, conv_b.reshape(1, d), wg, bg, lam.reshape(1, d), w_out,
      w_attn, w_o)


def _attn_kernel(sinks_ref, q_ref, k_ref, v_ref, kp_ref, vp_ref, ma_ref, gb_ref, x_ref,
                 mod_ref, wa_ref, wo_ref, wfi32_ref, wfo32_ref, o_ref, wfi16_ref, wfo16_ref, y_s):
    wfi16_ref[...] = wfi32_ref[...].astype(BF16)
    wfo16_ref[...] = wfo32_ref[...].astype(BF16)

    tq = q_ref.shape[0]
    n_blk = tq // WINDOW
    first = pl.program_id(1) == 0
    rows = GQA_GROUP * WINDOW

    qi = lax.broadcasted_iota(jnp.int32, (rows, WINDOW), 0) % WINDOW
    kk = lax.broadcasted_iota(jnp.int32, (rows, WINDOW), 1)
    in_cur = kk <= qi
    scale = HEAD_DIM ** -0.5
    nt_dims = (((1,), (1,)), ((), ()))
    ones_blk = jnp.ones((2 * WINDOW, HEAD_DIM), BF16)

    for h in range(N_KV_HEADS):
        hs = slice(h * HEAD_DIM, (h + 1) * HEAD_DIM)
        sink = jnp.concatenate(
            [jnp.full((WINDOW, 1), sinks_ref[h * GQA_GROUP + g], F32) for g in range(GQA_GROUP)],
            axis=0)
        for n in range(n_blk):
            rs = slice(n * WINDOW, (n + 1) * WINDOW)
            q4 = jnp.concatenate(
                [q_ref[rs, (h * GQA_GROUP + g) * HEAD_DIM:(h * GQA_GROUP + g + 1) * HEAD_DIM]
                 for g in range(GQA_GROUP)], axis=0)
            if n == 0:
                k_prev, v_prev = kp_ref[:, hs], vp_ref[:, hs]
            else:
                ps = slice((n - 1) * WINDOW, n * WINDOW)
                k_prev, v_prev = k_ref[ps, hs], v_ref[ps, hs]
            k_cat = jnp.concatenate([k_prev, k_ref[rs, hs]], axis=0)
            v_cat = jnp.concatenate([v_prev, v_ref[rs, hs]], axis=0)
            s = lax.dot_general(q4, k_cat, nt_dims, preferred_element_type=F32)
            s_prev = s[:, :WINDOW]
            if n == 0:
                s_prev = jnp.where(first, -jnp.inf, s_prev)
            sc = jnp.where(in_cur, s[:, WINDOW:], s_prev) * scale
            m = jnp.maximum(jnp.max(sc, axis=-1, keepdims=True), sink)
            p = jnp.exp(sc - m)
            p_cat = jnp.concatenate(
                [jnp.where(in_cur, 0.0, p), jnp.where(in_cur, p, 0.0)], axis=1).astype(BF16)
            o_ext = jnp.dot(p_cat, jnp.concatenate([v_cat, ones_blk], axis=1),
                            preferred_element_type=F32)
            denom = o_ext[:, HEAD_DIM:] + jnp.exp(sink - m)
            o = o_ext[:, :HEAD_DIM] / denom
            for g in range(GQA_GROUP):
                cs = slice((h * GQA_GROUP + g) * HEAD_DIM, (h * GQA_GROUP + g + 1) * HEAD_DIM)
                y_s[rs, cs] = o[g * WINDOW:(g + 1) * WINDOW].astype(BF16)

    attn = jnp.dot(y_s[...], wa_ref[...], preferred_element_type=F32)
    merged = gb_ref[...].astype(F32) * attn + ma_ref[...].astype(F32)
    r = jnp.dot(merged.astype(BF16), wo_ref[...], preferred_element_type=F32)
    o_ref[...] = x_ref[...] + mod_ref[0, 2:3, :] * r


def _attn_merge(proj, merged_a, x, sinks, mod, w_attn, w_o, w_ffn_in, w_ffn_out, layer,
                batch, seq, tq):
    t, d = x.shape
    per_seq = seq // tq
    n_blk = tq // WINDOW
    row = lambda b, s: b * per_seq + s
    prev_blk = lambda b, s: jnp.maximum(row(b, s) * n_blk - 1, 0)
    fi_in, fi_out, fi_shape = _cast_specs(w_ffn_in.shape, layer, row, batch * per_seq)
    fo_in, fo_out, fo_shape = _cast_specs(w_ffn_out.shape, layer, row, batch * per_seq)
    return pl.pallas_call(
        _attn_kernel,
        grid=(batch, per_seq),
        in_specs=[
            pl.BlockSpec(memory_space=pltpu.SMEM),
            pl.BlockSpec((tq, d), lambda b, s: (row(b, s), OUT_Q)),
            pl.BlockSpec((tq, KV_WIDTH), lambda b, s: (row(b, s), OUT_K)),
            pl.BlockSpec((tq, KV_WIDTH), lambda b, s: (row(b, s), OUT_V)),
            pl.BlockSpec((WINDOW, KV_WIDTH), lambda b, s: (prev_blk(b, s), OUT_K)),
            pl.BlockSpec((WINDOW, KV_WIDTH), lambda b, s: (prev_blk(b, s), OUT_V)),
            pl.BlockSpec((tq, d), lambda b, s: (row(b, s), 0)),
            pl.BlockSpec((tq, d), lambda b, s: (row(b, s), OUT_GB)),
            pl.BlockSpec((tq, d), lambda b, s: (row(b, s), 0)),
            pl.BlockSpec((1, N_MOD, d), lambda b, s: (b, 0, 0)),
            _resident((d, d), lambda b, s: (0, 0)),
            _resident((d, d), lambda b, s: (0, 0)),
            fi_in,
            fo_in,
        ],
        out_specs=[pl.BlockSpec((tq, d), lambda b, s: (row(b, s), 0)), fi_out, fo_out],
        out_shape=[jax.ShapeDtypeStruct((t, d), F32), fi_shape, fo_shape],
        scratch_shapes=[pltpu.VMEM((tq, d), BF16)],
        compiler_params=_params("arbitrary", "arbitrary"),
        name="swa_merge_out_proj",
    )(sinks, proj, proj, proj, proj, proj, merged_a, proj, x, mod, w_attn, w_o,
      w_ffn_in, w_ffn_out)


def _ffn_kernel(x_ref, mod_ref, g_ref, wg_ref, wu_ref, wo_ref, fg_ref, o_ref, h_ref,
                *, rows, final_norm):
    acc_ref = o_ref
    tm, d = x_ref.shape
    th = wg_ref.shape[1]
    j = pl.program_id(1)

    @pl.when(j == 0)
    def _():
        _norm_mod_into(x_ref, h_ref, g_ref[...] * (1.0 + mod_ref[0, 4:5, :]), mod_ref[0, 3:4, :],
                       rows, also_zero=acc_ref)

    h = h_ref[...]
    acts = []
    for c0 in range(0, th, 256):
        gate = jnp.dot(h, wg_ref[:, c0:c0 + 256], preferred_element_type=F32)
        up = jnp.dot(h, wu_ref[:, c0:c0 + 256], preferred_element_type=F32)
        acts.append((gate * jax.nn.sigmoid(gate) * up).astype(BF16))
    act = jnp.concatenate(acts, axis=1)
    for c0 in range(0, d, 512):
        acc_ref[:, c0:c0 + 512] += jnp.dot(act, wo_ref[:, c0:c0 + 512],
                                            preferred_element_type=F32)

    @pl.when(j == pl.num_programs(1) - 1)
    def _():
        def body(i, carry):
            r = pl.ds(pl.multiple_of(i * rows, rows), rows)
            y = x_ref[r, :] + mod_ref[0, 5:6, :] * acc_ref[r, :]
            if final_norm:
                inv = lax.rsqrt(jnp.mean(y * y, axis=-1, keepdims=True) + EPS)
                y = (y * inv) * fg_ref[...]
            o_ref[r, :] = y
            return carry
        lax.fori_loop(0, tm // rows, body, 0)


def _ffn(x, mod, g, w_in, w_out, final_g, final_norm, seq, tm, th):
    t, d = x.shape
    hidden = w_out.shape[0]
    per_seq = seq // tm
    n_h = hidden // th
    return pl.pallas_call(
        functools.partial(_ffn_kernel, rows=min(tm, 128), final_norm=final_norm),
        grid=(t // tm, n_h),
        in_specs=[
            pl.BlockSpec((tm, d), lambda i, j: (i, 0)),
            pl.BlockSpec((1, N_MOD, d), lambda i, j: (i // per_seq, 0, 0)),
            pl.BlockSpec((1, d), lambda i, j: (0, 0)),
            pl.BlockSpec((d, th), lambda i, j: (0, j)),
            pl.BlockSpec((d, th), lambda i, j: (0, j + n_h)),
            pl.BlockSpec((th, d), lambda i, j: (j, 0)),
            pl.BlockSpec((1, d), lambda i, j: (0, 0)),
        ],
        out_specs=pl.BlockSpec((tm, d), lambda i, j: (i, 0)),
        out_shape=jax.ShapeDtypeStruct((t, d), F32),
        scratch_shapes=[pltpu.VMEM((tm, d), BF16)],
        compiler_params=_params("arbitrary", "arbitrary"),
        name="norm_swiglu",
    )(x, mod, g.reshape(1, d), w_in, w_in, w_out, final_g.reshape(1, d))


def kernel(x, c, ada_w, ada_b, norm1_g, w_in, b_in, conv_w, conv_b, lru_wa, lru_ba, lru_wx,
           lru_bx, lru_lambda, sinks, w_lru_out, w_attn_out, w_o, norm2_g, w_ffn_in, w_ffn_out,
           final_g):
    batch, seq, d = x.shape
    assert d == D_MODEL and seq % WINDOW == 0
    depth = ada_w.shape[0]
    t = batch * seq
    tm_proj = min(seq, 1024)
    tm_ffn = min(seq, 1024)
    ts_lru = min(seq, 256)
    tq_attn = min(seq, 256)

    wg =jnp.concatenate([lru_wa, lru_wx], axis=-1).astype(BF16)
    bg = jnp.concatenate([lru_ba.reshape(depth, LRU_HEADS, 1, LRU_HEAD_DIM),
                          lru_bx.reshape(depth, LRU_HEADS, 1, LRU_HEAD_DIM)], axis=-1)
    b_in3 = b_in.reshape(depth, 1, C_IN)

    mod = _modulation(c, ada_w, ada_b)
    xf = x.reshape(t, d)
    for l in range(depth):
        proj = _in_proj(xf, mod[l], norm1_g[l], w_in, b_in3, l, seq, tm_proj)
        merged_a, w_attn_bf, w_o_bf = _lru_branch(
            proj, conv_w[l], conv_b[l], wg[l], bg[l], lru_lambda[l], w_lru_out, w_attn_out, w_o,
            l, batch, seq, ts_lru)
        xf, w_ffn_in_bf, w_ffn_out_bf = _attn_merge(
            proj, merged_a, xf, sinks[l], mod[l], w_attn_bf, w_o_bf, w_ffn_in, w_ffn_out, l,
            batch, seq, tq_attn)
        xf = _ffn(xf, mod[l], norm2_g[l], w_ffn_in_bf, w_ffn_out_bf, final_g,
                  l == depth - 1, seq, tm_ffn, 512)
    return xf.reshape(batch, seq, d)
```

```python
import functools

import jax
import jax.numpy as jnp
from jax import lax
from jax.experimental import pallas as pl
from jax.experimental.pallas import tpu as pltpu

F32 = jnp.float32
BF16 = jnp.bfloat16

D_MODEL = 2048
LRU_HEADS = 16
LRU_HEAD_DIM = D_MODEL // LRU_HEADS
CONV_W = 4
LRU_C = 8.0
HEAD_DIM = 128
N_Q_HEADS = 16
N_KV_HEADS = 4
GQA_GROUP = N_Q_HEADS // N_KV_HEADS
WINDOW = 128
KV_WIDTH = N_KV_HEADS * HEAD_DIM
N_MOD = 6
EPS = 1e-6
C_IN = 5 * D_MODEL + 2 * KV_WIDTH

PROJ_TN = 2 * KV_WIDTH
_BLK = D_MODEL // PROJ_TN
_KV_BLK = 2 * KV_WIDTH // PROJ_TN
OUT_U, OUT_GATE, OUT_Q, OUT_GA, OUT_GB = 0, 1, 2, 3, 4
OUT_K = 5 * D_MODEL // KV_WIDTH
OUT_V = OUT_K + 1
N_PROJ_BLK = C_IN // PROJ_TN

SUBLANES = 8
LANES = 128
VMEM_LIMIT = 56 * 1024 * 1024


def _src_col_block(j):
    return jnp.where(j < 3 * _BLK, j, jnp.where(j < 5 * _BLK, j + _KV_BLK, j - 2 * _BLK))


def _params(*sem):
    return pltpu.CompilerParams(dimension_semantics=sem, vmem_limit_bytes=VMEM_LIMIT)


def _cast_plan(rows, n_steps):
    k = n_steps
    while rows % k or (rows // k) % (2 * SUBLANES):
        k -= 1
    return rows // k, k


def _cast_specs(shape, layer, step_of, n_steps):
    rows, cols = shape[1:]
    blk, n_blk = _cast_plan(rows, n_steps)
    walk = lambda *g: jnp.minimum(step_of(*g), n_blk - 1)
    return (pl.BlockSpec((None, blk, cols), lambda *g: (layer, walk(*g), 0)),
            pl.BlockSpec((blk, cols), lambda *g: (walk(*g), 0)),
            jax.ShapeDtypeStruct((rows, cols), BF16))


def _resident(shape, index_map):
    return pl.BlockSpec(shape, index_map, pipeline_mode=pl.Buffered(1))


def _norm_mod_into(x_ref, h_ref, gain, shift, rows, also_zero=None):
    tm, d = x_ref.shape
    group = 2 * SUBLANES

    def body(i, carry):
        for q in range(rows // group):
            r = pl.ds(pl.multiple_of(i * rows + q * group, group), group)
            x = x_ref[r, :]
            inv = lax.rsqrt(jnp.mean(x * x, axis=-1, keepdims=True) + EPS)
            h_ref[r, :] = ((x * inv) * gain + shift).astype(BF16)
            if also_zero is not None:
                also_zero[r, :] = jnp.zeros((group, d), F32)
        return carry

    lax.fori_loop(0, tm // rows, body, 0)


def _gelu_tanh(x):
    return 0.5 * x * (1.0 + jnp.tanh(0.7978845608028654 * (x + 0.044715 * (x * x * x))))


def _softplus(z):
    return jnp.maximum(z, 0.0) + jnp.log1p(jnp.exp(-jnp.abs(z)))


def _mod_kernel(c_ref, w_ref, b_ref, o_ref):
    c = c_ref[...]
    act = (c * jax.nn.sigmoid(c)).astype(BF16)
    o_ref[0] = jnp.dot(act, w_ref[0].astype(BF16), preferred_element_type=F32) + b_ref[0]


def _modulation(c, ada_w, ada_b, tn=1024):
    depth, d, n = ada_w.shape
    b = c.shape[0]
    rows = -(-b // SUBLANES) * SUBLANES
    c_pad = jnp.zeros((rows, d), F32).at[:b].set(c)
    out = pl.pallas_call(
        _mod_kernel,
        grid=(depth, n // tn),
        in_specs=[
            pl.BlockSpec((rows, d), lambda l, j: (0, 0)),
            pl.BlockSpec((1, d, tn), lambda l, j: (l, 0, j)),
            pl.BlockSpec((1, 1, tn), lambda l, j: (l, 0, j)),
        ],
        out_specs=pl.BlockSpec((1, rows, tn), lambda l, j: (l, 0, j)),
        out_shape=jax.ShapeDtypeStruct((depth, rows, n), F32),
        compiler_params=_params("arbitrary", "arbitrary"),
        name="adaln_modulation",
    )(c_pad, ada_w, ada_b.reshape(depth, 1, n))
    return out[:, :b].reshape(depth, b, N_MOD, d)


def _inproj_kernel(x_ref, mod_ref, g_ref, w_ref, b_ref, o_ref, h_ref, *, rows, chunk):
    tm = x_ref.shape[0]
    j = pl.program_id(1)

    @pl.when(j == 0)
    def _():
        _norm_mod_into(x_ref, h_ref, g_ref[...] * (1.0 + mod_ref[0, 1:2, :]), mod_ref[0, 0:1, :],
                       rows)

    def project(act):
        for c0 in range(0, w_ref.shape[1], chunk):
            cols = slice(c0, c0 + chunk)
            z = jnp.dot(h_ref[...], w_ref[:, cols].astype(BF16),
                        preferred_element_type=F32) + b_ref[:, cols]
            o_ref[:, cols] = act(z).astype(o_ref.dtype)

    blk = j // _BLK
    is_gelu = blk == OUT_GATE
    is_sigmoid = jnp.logical_or(blk == OUT_GA, blk == OUT_GB)

    @pl.when(is_gelu)
    def _():
        project(_gelu_tanh)

    @pl.when(is_sigmoid)
    def _():
        project(jax.nn.sigmoid)

    @pl.when(jnp.logical_not(jnp.logical_or(is_gelu, is_sigmoid)))
    def _():
        project(lambda z: z)


def _in_proj(x, mod, g, w, b, layer, seq, tm):
    t, d = x.shape
    tn = PROJ_TN
    per_seq = seq // tm
    if w.ndim == 3:
        w_spec = pl.BlockSpec((None, d, tn), lambda i, j: (layer, 0, _src_col_block(j)))
    else:
        w_spec = pl.BlockSpec((d, tn), lambda i, j: (0, _src_col_block(j)))
    return pl.pallas_call(
        functools.partial(_inproj_kernel, rows=min(tm, 128), chunk=PROJ_TN),
        grid=(t // tm, N_PROJ_BLK),
        in_specs=[
            pl.BlockSpec((tm, d), lambda i, j: (i, 0)),
            pl.BlockSpec((1, N_MOD, d), lambda i, j: (i // per_seq, 0, 0)),
            pl.BlockSpec((1, d), lambda i, j: (0, 0)),
            w_spec,
            pl.BlockSpec((None, 1, tn), lambda i, j: (layer, 0, _src_col_block(j))),
        ],
        out_specs=pl.BlockSpec((tm, tn), lambda i, j: (i, j)),
        out_shape=jax.ShapeDtypeStruct((t, C_IN), BF16),
        scratch_shapes=[pltpu.VMEM((tm, d), BF16)],
        compiler_params=_params("arbitrary", "arbitrary"),
        name="norm_in_proj",
    )(x, mod, g.reshape(1, d), w, b)


def _lru_kernel(u_ref, gate_ref, mg_ref, convw_ref, convb_ref, wg_ref, bg_ref, lam_ref,
                wout_ref, wa32_ref, wo32_ref, o_ref, wa16_ref, wo16_ref,
                ubuf, a_s, b_s, y_s, hcarry, wout_bf, *, group, per_seq):
    wa16_ref[...] = wa32_ref[...].astype(BF16)
    wo16_ref[...] = wo32_ref[...].astype(BF16)
    ts = u_ref.shape[0]
    sub = ts // SUBLANES
    pitch = sub + 4
    halo = SUBLANES
    g_idx = pl.program_id(0)
    first = g_idx % per_seq == 0

    @pl.when(g_idx == 0)
    def _():
        ubuf[:, ts:ts + halo, :] = jnp.zeros((LRU_HEADS, halo, LANES), F32)
        hcarry[...] = jnp.zeros_like(hcarry)
        y_s[...] = jnp.zeros_like(y_s)
        wout_bf[...] = wout_ref[...].astype(BF16)

    ubuf[:, 0:halo, :] = jnp.where(first, 0.0, ubuf[:, ts:ts + halo, :])
    hcarry[...] = jnp.where(first, 0.0, hcarry[...])

    neg_c_softplus = -LRU_C * _softplus(-lam_ref[...])
    out_chunks = 8
    out_cols = D_MODEL // out_chunks
    heads_per_chunk = LRU_HEADS // out_chunks

    for hd in range(LRU_HEADS):
        if hd % heads_per_chunk == 0:
            c0 = (hd // heads_per_chunk) * out_cols
            out = jnp.dot(y_s[...], wout_bf[:, c0:c0 + out_cols], preferred_element_type=F32)
            o_ref[:, c0:c0 + out_cols] = (
                mg_ref[:, c0:c0 + out_cols].astype(F32) * out).astype(o_ref.dtype)
        cs = slice(hd * LRU_HEAD_DIM, (hd + 1) * LRU_HEAD_DIM)
        ubuf[hd, halo:halo + ts, :] = u_ref[:, cs].astype(F32)
        uc = convb_ref[:, cs]
        for k in range(CONV_W):
            off = halo - (CONV_W - 1) + k
            uc = uc + convw_ref[k:k + 1, cs] * ubuf[hd, off:off + ts, :]
        g = jnp.dot(uc.astype(BF16), wg_ref[hd], preferred_element_type=F32) + bg_ref[hd]
        r = jax.nn.sigmoid(g[:, :LRU_HEAD_DIM])
        i = jax.nn.sigmoid(g[:, LRU_HEAD_DIM:])
        log_a = r * neg_c_softplus[:, cs]
        a = jnp.exp(log_a)
        one_m_a2 = (1.0 + a * a) * jnp.tanh(-log_a)
        beta = jnp.where(one_m_a2 > 0.0, one_m_a2 * lax.rsqrt(one_m_a2), 0.0)
        b = beta * (i * uc)
        for s in range(SUBLANES):
            a_s[hd, s * pitch:s * pitch + sub, :] = a[s * sub:(s + 1) * sub, :]
            b_s[hd, s * pitch:s * pitch + sub, :] = b[s * sub:(s + 1) * sub, :]

    row = lax.broadcasted_iota(jnp.int32, (SUBLANES, LANES), 0)
    zeros = jnp.zeros((SUBLANES, LANES), F32)
    ones = jnp.ones((SUBLANES, LANES), F32)

    for h0 in range(0, LRU_HEADS, group):
        heads = range(h0, h0 + group)

        def strided(ref, hd, m):
            return ref[hd, pl.ds(m, SUBLANES, stride=pitch), :]

        def pass1(m, carry):
            out = []
            for n, hd in enumerate(heads):
                a_m = strided(a_s, hd, m)
                out += [a_m * carry[2 * n] + strided(b_s, hd, m), carry[2 * n + 1] * a_m]
            return tuple(out)

        ends = lax.fori_loop(0, sub, pass1, (zeros, ones) * group, unroll=2)

        starts = []
        for n, hd in enumerate(heads):
            e, q = ends[2 * n], ends[2 * n + 1]
            for d in (1, 2, 4):
                keep = row >= d
                e_prev = pltpu.roll(e, d, 0)
                q_prev = pltpu.roll(q, d, 0)
                e = jnp.where(keep, q * e_prev, 0.0) + e
                q = jnp.where(keep, q * q_prev, q)
            cs = slice(hd * LRU_HEAD_DIM, (hd + 1) * LRU_HEAD_DIM)
            h_in = hcarry[:, cs]
            end_state = e + q * h_in
            hcarry[:, cs] = end_state[SUBLANES - 1:SUBLANES, :]
            starts.append(jnp.where(row == 0, h_in, pltpu.roll(end_state, 1, 0)))

        def pass2(m, carry):
            out = []
            for n, hd in enumerate(heads):
                h = strided(a_s, hd, m) * carry[n] + strided(b_s, hd, m)
                b_s[hd, pl.ds(m, SUBLANES, stride=pitch), :] = h
                out.append(h)
            return tuple(out)

        lax.fori_loop(0, sub, pass2, tuple(starts), unroll=2)

    for hd in range(LRU_HEADS):
        cs = slice(hd * LRU_HEAD_DIM, (hd + 1) * LRU_HEAD_DIM)
        h = jnp.concatenate(
            [b_s[hd, s * pitch:s * pitch + sub, :] for s in range(SUBLANES)], axis=0)
        y_s[:, cs] = (h * gate_ref[:, cs].astype(F32)).astype(BF16)


def _lru_branch(proj, conv_w, conv_b, wg, bg, lam, w_out, w_attn, w_o, layer, batch, seq, ts):
    t = proj.shape[0]
    d = D_MODEL
    per_seq = seq // ts
    n_tiles = batch * per_seq
    pitch = ts // SUBLANES + 4
    cur = lambda g: jnp.minimum(g, n_tiles - 1)
    prev = lambda g: jnp.maximum(g - 1, 0)
    wa_in, wa_out, wa_shape = _cast_specs(w_attn.shape, layer, lambda g: g, n_tiles + 1)
    wo_in, wo_out, wo_shape = _cast_specs(w_o.shape, layer, lambda g: g, n_tiles + 1)
    return pl.pallas_call(
        functools.partial(_lru_kernel, group=8, per_seq=per_seq),
        grid=(n_tiles + 1,),
        in_specs=[
            pl.BlockSpec((ts, d), lambda g: (cur(g), OUT_U)),
            pl.BlockSpec((ts, d), lambda g: (cur(g), OUT_GATE)),
            pl.BlockSpec((ts, d), lambda g: (prev(g), OUT_GA)),
            _resident((CONV_W, d), lambda g: (0, 0)),
            _resident((1, d), lambda g: (0, 0)),
            _resident((LRU_HEADS, LRU_HEAD_DIM, 2 * LRU_HEAD_DIM), lambda g: (0, 0, 0)),
            _resident((LRU_HEADS, 1, 2 * LRU_HEAD_DIM), lambda g: (0, 0, 0)),
            _resident((1, d), lambda g: (0, 0)),
            _resident((None, d, d), lambda g: (layer, 0, 0)),
            wa_in,
            wo_in,
        ],
        out_specs=[pl.BlockSpec((ts, d), lambda g: (prev(g), 0)), wa_out, wo_out],
        out_shape=[jax.ShapeDtypeStruct((t, d), BF16), wa_shape, wo_shape],
        scratch_shapes=[
            pltpu.VMEM((LRU_HEADS, ts + SUBLANES, LANES), F32),
            pltpu.VMEM((LRU_HEADS, SUBLANES * pitch, LANES), F32),
            pltpu.VMEM((LRU_HEADS, SUBLANES * pitch, LANES), F32),
            pltpu.VMEM((ts, d), BF16),
            pltpu.VMEM((1, d), F32),
            pltpu.VMEM((d, d), BF16),
        ],
        compiler_params=_params("arbitrary"),
        name="rglru_branch",
    )(proj, proj, proj, conv_w, conv_b.reshape(1, d), wg, bg, lam.reshape(1, d), w_out,
      w_attn, w_o)


def _attn_kernel(sinks_ref, q_ref, k_ref, v_ref, kp_ref, vp_ref, ma_ref, gb_ref, x_ref,
                 mod_ref, wa_ref, wo_ref, wfi32_ref, wfo32_ref, o_ref, wfi16_ref, wfo16_ref, y_s):
    wfi16_ref[...] = wfi32_ref[...].astype(BF16)
    wfo16_ref[...] = wfo32_ref[...].astype(BF16)

    tq = q_ref.shape[0]
    n_blk = tq // WINDOW
    first = pl.program_id(1) == 0
    rows = GQA_GROUP * WINDOW

    qi = lax.broadcasted_iota(jnp.int32, (rows, WINDOW), 0) % WINDOW
    kk = lax.broadcasted_iota(jnp.int32, (rows, WINDOW), 1)
    in_cur = kk <= qi
    scale = HEAD_DIM ** -0.5
    nt_dims = (((1,), (1,)), ((), ()))
    ones_blk = jnp.ones((2 * WINDOW, HEAD_DIM), BF16)

    for h in range(N_KV_HEADS):
        hs = slice(h * HEAD_DIM, (h + 1) * HEAD_DIM)
        sink = jnp.concatenate(
            [jnp.full((WINDOW, 1), sinks_ref[h * GQA_GROUP + g], F32) for g in range(GQA_GROUP)],
            axis=0)
        for n in range(n_blk):
            rs = slice(n * WINDOW, (n + 1) * WINDOW)
            q4 = jnp.concatenate(
                [q_ref[rs, (h * GQA_GROUP + g) * HEAD_DIM:(h * GQA_GROUP + g + 1) * HEAD_DIM]
                 for g in range(GQA_GROUP)], axis=0)
            if n == 0:
                k_prev, v_prev = kp_ref[:, hs], vp_ref[:, hs]
            else:
                ps = slice((n - 1) * WINDOW, n * WINDOW)
                k_prev, v_prev = k_ref[ps, hs], v_ref[ps, hs]
            k_cat = jnp.concatenate([k_prev, k_ref[rs, hs]], axis=0)
            v_cat = jnp.concatenate([v_prev, v_ref[rs, hs]], axis=0)
            s = lax.dot_general(q4, k_cat, nt_dims, preferred_element_type=F32)
            s_prev = s[:, :WINDOW]
            if n == 0:
                s_prev = jnp.where(first, -jnp.inf, s_prev)
            sc = jnp.where(in_cur, s[:, WINDOW:], s_prev) * scale
            m = jnp.maximum(jnp.max(sc, axis=-1, keepdims=True), sink)
            p = jnp.exp(sc - m)
            p_cat = jnp.concatenate(
                [jnp.where(in_cur, 0.0, p), jnp.where(in_cur, p, 0.0)], axis=1).astype(BF16)
            o_ext = jnp.dot(p_cat, jnp.concatenate([v_cat, ones_blk], axis=1),
                            preferred_element_type=F32)
            denom = o_ext[:, HEAD_DIM:] + jnp.exp(sink - m)
            o = o_ext[:, :HEAD_DIM] / denom
            for g in range(GQA_GROUP):
                cs = slice((h * GQA_GROUP + g) * HEAD_DIM, (h * GQA_GROUP + g + 1) * HEAD_DIM)
                y_s[rs, cs] = o[g * WINDOW:(g + 1) * WINDOW].astype(BF16)

    attn = jnp.dot(y_s[...], wa_ref[...], preferred_element_type=F32)
    merged = gb_ref[...].astype(F32) * attn + ma_ref[...].astype(F32)
    r = jnp.dot(merged.astype(BF16), wo_ref[...], preferred_element_type=F32)
    o_ref[...] = x_ref[...] + mod_ref[0, 2:3, :] * r


def _attn_merge(proj, merged_a, x, sinks, mod, w_attn, w_o, w_ffn_in, w_ffn_out, layer,
                batch, seq, tq):
    t, d = x.shape
    per_seq = seq // tq
    n_blk = tq // WINDOW
    row = lambda b, s: b * per_seq + s
    prev_blk = lambda b, s: jnp.maximum(row(b, s) * n_blk - 1, 0)
    fi_in, fi_out, fi_shape = _cast_specs(w_ffn_in.shape, layer, row, batch * per_seq)
    fo_in, fo_out, fo_shape = _cast_specs(w_ffn_out.shape, layer, row, batch * per_seq)
    return pl.pallas_call(
        _attn_kernel,
        grid=(batch, per_seq),
        in_specs=[
            pl.BlockSpec(memory_space=pltpu.SMEM),
            pl.BlockSpec((tq, d), lambda b, s: (row(b, s), OUT_Q)),
            pl.BlockSpec((tq, KV_WIDTH), lambda b, s: (row(b, s), OUT_K)),
            pl.BlockSpec((tq, KV_WIDTH), lambda b, s: (row(b, s), OUT_V)),
            pl.BlockSpec((WINDOW, KV_WIDTH), lambda b, s: (prev_blk(b, s), OUT_K)),
            pl.BlockSpec((WINDOW, KV_WIDTH), lambda b, s: (prev_blk(b, s), OUT_V)),
            pl.BlockSpec((tq, d), lambda b, s: (row(b, s), 0)),
            pl.BlockSpec((tq, d), lambda b, s: (row(b, s), OUT_GB)),
            pl.BlockSpec((tq, d), lambda b, s: (row(b, s), 0)),
            pl.BlockSpec((1, N_MOD, d), lambda b, s: (b, 0, 0)),
            _resident((d, d), lambda b, s: (0, 0)),
            _resident((d, d), lambda b, s: (0, 0)),
            fi_in,
            fo_in,
        ],
        out_specs=[pl.BlockSpec((tq, d), lambda b, s: (row(b, s), 0)), fi_out, fo_out],
        out_shape=[jax.ShapeDtypeStruct((t, d), F32), fi_shape, fo_shape],
        scratch_shapes=[pltpu.VMEM((tq, d), BF16)],
        compiler_params=_params("arbitrary", "arbitrary"),
        name="swa_merge_out_proj",
    )(sinks, proj, proj, proj, proj, proj, merged_a, proj, x, mod, w_attn, w_o,
      w_ffn_in, w_ffn_out)


def _ffn_kernel(x_ref, mod_ref, g_ref, wg_ref, wu_ref, wo_ref, fg_ref, *rest, rows, final_norm,
                side_cast):
    if side_cast:
        w32_ref, o_ref, w16_ref, h_ref = rest
        w16_ref[...] = w32_ref[...].astype(BF16)
    else:
        o_ref, h_ref = rest
    acc_ref = o_ref
    tm, d = x_ref.shape
    th = wg_ref.shape[1]
    j = pl.program_id(1)

    @pl.when(j == 0)
    def _():
        _norm_mod_into(x_ref, h_ref, g_ref[...] * (1.0 + mod_ref[0, 4:5, :]), mod_ref[0, 3:4, :],
                       rows, also_zero=acc_ref)

    h = h_ref[...]
    acts = []
    for c0 in range(0, th, 256):
        gate = jnp.dot(h, wg_ref[:, c0:c0 + 256], preferred_element_type=F32)
        up = jnp.dot(h, wu_ref[:, c0:c0 + 256], preferred_element_type=F32)
        acts.append((gate * jax.nn.sigmoid(gate) * up).astype(BF16))
    act = jnp.concatenate(acts, axis=1)
    for c0 in range(0, d, 512):
        acc_ref[:, c0:c0 + 512] += jnp.dot(act, wo_ref[:, c0:c0 + 512],
                                            preferred_element_type=F32)

    @pl.when(j == pl.num_programs(1) - 1)
    def _():
        def body(i, carry):
            r = pl.ds(pl.multiple_of(i * rows, rows), rows)
            y = x_ref[r, :] + mod_ref[0, 5:6, :] * acc_ref[r, :]
            if final_norm:
                inv = lax.rsqrt(jnp.mean(y * y, axis=-1, keepdims=True) + EPS)
                y = (y * inv) * fg_ref[...]
            o_ref[r, :] = y
            return carry
        lax.fori_loop(0, tm // rows, body, 0)


def _ffn(x, mod, g, w_in, w_out, final_g, final_norm, seq, tm, th, w_next=None, next_layer=None):
    t, d = x.shape
    hidden = w_out.shape[0]
    per_seq = seq // tm
    n_h = hidden // th
    n_tiles = t // tm
    in_specs = [
        pl.BlockSpec((tm, d), lambda i, j: (i, 0)),
        pl.BlockSpec((1, N_MOD, d), lambda i, j: (i // per_seq, 0, 0)),
        pl.BlockSpec((1, d), lambda i, j: (0, 0)),
        pl.BlockSpec((d, th), lambda i, j: (0, j)),
        pl.BlockSpec((d, th), lambda i, j: (0, j + n_h)),
        pl.BlockSpec((th, d), lambda i, j: (j, 0)),
        pl.BlockSpec((1, d), lambda i, j: (0, 0)),
    ]
    out_specs = [pl.BlockSpec((tm, d), lambda i, j: (i, 0))]
    out_shape = [jax.ShapeDtypeStruct((t, d), F32)]
    args = [x, mod, g.reshape(1, d), w_in, w_in, w_out, final_g.reshape(1, d)]
    side_cast = w_next is not None
    if side_cast:
        c_in, c_out, c_shape = _cast_specs(w_next.shape, next_layer, lambda i, j: i * n_h + j,
                                           n_tiles * n_h)
        in_specs.append(c_in)
        out_specs.append(c_out)
        out_shape.append(c_shape)
        args.append(w_next)
    res = pl.pallas_call(
        functools.partial(_ffn_kernel, rows=min(tm, 128), final_norm=final_norm,
                          side_cast=side_cast),
        grid=(n_tiles, n_h),
        in_specs=in_specs,
        out_specs=out_specs,
        out_shape=out_shape,
        scratch_shapes=[pltpu.VMEM((tm, d), BF16)],
        compiler_params=_params("arbitrary", "arbitrary"),
        name="norm_swiglu",
    )(*args)
    return (res[0], res[1]) if side_cast else (res[0], None)


def kernel(x, c, ada_w, ada_b, norm1_g, w_in, b_in, conv_w, conv_b, lru_wa, lru_ba, lru_wx,
           lru_bx, lru_lambda, sinks, w_lru_out, w_attn_out, w_o, norm2_g, w_ffn_in, w_ffn_out,
           final_g):
    batch, seq, d = x.shape
    assert d == D_MODEL and seq % WINDOW == 0
    depth = ada_w.shape[0]
    t = batch * seq
    tm_proj = min(seq, 1024)
    tm_ffn = min(seq, 1024)
    ts_lru = min(seq, 256)
    tq_attn = min(seq, 256)

    wg =jnp.concatenate([lru_wa, lru_wx], axis=-1).astype(BF16)
    bg = jnp.concatenate([lru_ba.reshape(depth, LRU_HEADS, 1, LRU_HEAD_DIM),
                          lru_bx.reshape(depth, LRU_HEADS, 1, LRU_HEAD_DIM)], axis=-1)
    b_in3 = b_in.reshape(depth, 1, C_IN)

    mod = _modulation(c, ada_w, ada_b)
    xf = x.reshape(t, d)
    w_in_l = w_in
    for l in range(depth):
        last = l == depth - 1
        proj = _in_proj(xf, mod[l], norm1_g[l], w_in_l, b_in3, l, seq, tm_proj)
        merged_a, w_attn_bf, w_o_bf = _lru_branch(
            proj, conv_w[l], conv_b[l], wg[l], bg[l], lru_lambda[l], w_lru_out, w_attn_out, w_o,
            l, batch, seq, ts_lru)
        xf, w_ffn_in_bf, w_ffn_out_bf = _attn_merge(
            proj, merged_a, xf, sinks[l], mod[l], w_attn_bf, w_o_bf, w_ffn_in, w_ffn_out, l,
            batch, seq, tq_attn)
        xf, w_in_l = _ffn(xf, mod[l], norm2_g[l], w_ffn_in_bf, w_ffn_out_bf, final_g, last, seq,
                          tm_ffn, 512, w_next=None if last else w_in,
                          next_layer=None if last else l + 1)
    return xf.reshape(batch, seq, d)
```

```python
import functools

import jax
import jax.numpy as jnp
from jax import lax
from jax.experimental import pallas as pl
from jax.experimental.pallas import tpu as pltpu

F32 = jnp.float32
BF16 = jnp.bfloat16

D_MODEL = 2048
LRU_HEADS = 16
LRU_HEAD_DIM = D_MODEL // LRU_HEADS
CONV_W = 4
LRU_C = 8.0
HEAD_DIM = 128
N_Q_HEADS = 16
N_KV_HEADS = 4
GQA_GROUP = N_Q_HEADS // N_KV_HEADS
WINDOW = 128
KV_WIDTH = N_KV_HEADS * HEAD_DIM
N_MOD = 6
EPS = 1e-6
C_IN = 5 * D_MODEL + 2 * KV_WIDTH

PROJ_TN = 2 * KV_WIDTH
_BLK = D_MODEL // PROJ_TN
_KV_BLK = 2 * KV_WIDTH // PROJ_TN
OUT_U, OUT_GATE, OUT_Q, OUT_GA, OUT_GB = 0, 1, 2, 3, 4
OUT_K = 5 * D_MODEL // KV_WIDTH
OUT_V = OUT_K + 1
N_PROJ_BLK = C_IN // PROJ_TN

SUBLANES = 8
LANES = 128
VMEM_LIMIT = 56 * 1024 * 1024


def _src_col_block(j):
    return jnp.where(j < 3 * _BLK, j, jnp.where(j < 5 * _BLK, j + _KV_BLK, j - 2 * _BLK))


def _params(*sem):
    return pltpu.CompilerParams(dimension_semantics=sem, vmem_limit_bytes=VMEM_LIMIT)


def _cast_plan(rows, n_steps):
    k = n_steps
    while rows % k or (rows // k) % (2 * SUBLANES):
        k -= 1
    return rows // k, k


def _cast_specs(shape, layer, step_of, n_steps):
    rows, cols = shape[1:]
    blk, n_blk = _cast_plan(rows, n_steps)
    walk = lambda *g: jnp.minimum(step_of(*g), n_blk - 1)
    return (pl.BlockSpec((None, blk, cols), lambda *g: (layer, walk(*g), 0)),
            pl.BlockSpec((blk, cols), lambda *g: (walk(*g), 0)),
            jax.ShapeDtypeStruct((rows, cols), BF16))


def _resident(shape, index_map):
    return pl.BlockSpec(shape, index_map, pipeline_mode=pl.Buffered(1))


def _norm_mod_into(x_ref, h_ref, gain, shift, rows, also_zero=None):
    tm, d = x_ref.shape
    group = 2 * SUBLANES

    def body(i, carry):
        for q in range(rows // group):
            r = pl.ds(pl.multiple_of(i * rows + q * group, group), group)
            x = x_ref[r, :]
            inv = lax.rsqrt(jnp.mean(x * x, axis=-1, keepdims=True) + EPS)
            h_ref[r, :] = ((x * inv) * gain + shift).astype(BF16)
            if also_zero is not None:
                also_zero[r, :] = jnp.zeros((group, d), F32)
        return carry

    lax.fori_loop(0, tm // rows, body, 0)


_SQRT_2_OVER_PI = 0.7978845608028654


def _gelu_tanh(x):
    hx = 0.5 * x
    return hx + hx * jnp.tanh(x * (_SQRT_2_OVER_PI + (_SQRT_2_OVER_PI * 0.044715) * (x * x)))


def _sigmoid_tanh(x):
    return 0.5 * jnp.tanh(0.5 * x) + 0.5


def _softplus(z):
    return jnp.maximum(z, 0.0) + jnp.log1p(jnp.exp(-jnp.abs(z)))


def _mod_kernel(c_ref, w_ref, b_ref, o_ref):
    c = c_ref[...]
    act = (c * jax.nn.sigmoid(c)).astype(BF16)
    o_ref[0] = jnp.dot(act, w_ref[0].astype(BF16), preferred_element_type=F32) + b_ref[0]


def _modulation(c, ada_w, ada_b, tn=1024):
    depth, d, n = ada_w.shape
    b = c.shape[0]
    rows = -(-b // SUBLANES) * SUBLANES
    c_pad = jnp.zeros((rows, d), F32).at[:b].set(c)
    out = pl.pallas_call(
        _mod_kernel,
        grid=(depth, n // tn),
        in_specs=[
            pl.BlockSpec((rows, d), lambda l, j: (0, 0)),
            pl.BlockSpec((1, d, tn), lambda l, j: (l, 0, j)),
            pl.BlockSpec((1, 1, tn), lambda l, j: (l, 0, j)),
        ],
        out_specs=pl.BlockSpec((1, rows, tn), lambda l, j: (l, 0, j)),
        out_shape=jax.ShapeDtypeStruct((depth, rows, n), F32),
        compiler_params=_params("arbitrary", "arbitrary"),
        name="adaln_modulation",
    )(c_pad, ada_w, ada_b.reshape(depth, 1, n))
    return out[:, :b].reshape(depth, b, N_MOD, d)


def _inproj_kernel(x_ref, mod_ref, g_ref, w_ref, b_ref, o_ref, h_ref, *, rows):
    tm = x_ref.shape[0]
    j = pl.program_id(1)

    @pl.when(j == 0)
    def _():
        _norm_mod_into(x_ref, h_ref, g_ref[...] * (1.0 + mod_ref[0, 1:2, :]), mod_ref[0, 0:1, :],
                       rows)

    def project(act):
        z = jnp.dot(h_ref[...], w_ref[...].astype(BF16), preferred_element_type=F32) + b_ref[...]
        o_ref[...] = act(z).astype(o_ref.dtype)

    blk = j // _BLK
    is_gelu = blk == OUT_GATE
    is_sigmoid = jnp.logical_or(blk == OUT_GA, blk == OUT_GB)

    @pl.when(is_gelu)
    def _():
        project(_gelu_tanh)

    @pl.when(is_sigmoid)
    def _():
        project(_sigmoid_tanh)

    @pl.when(jnp.logical_not(jnp.logical_or(is_gelu, is_sigmoid)))
    def _():
        project(lambda z: z)


def _in_proj(x, mod, g, w, b, layer, seq, tm):
    t, d = x.shape
    tn = PROJ_TN
    per_seq = seq // tm
    if w.ndim == 3:
        w_spec = pl.BlockSpec((None, d, tn), lambda i, j: (layer, 0, _src_col_block(j)))
    else:
        w_spec = pl.BlockSpec((d, tn), lambda i, j: (0, _src_col_block(j)))
    return pl.pallas_call(
        functools.partial(_inproj_kernel, rows=min(tm, 128)),
        grid=(t // tm, N_PROJ_BLK),
        in_specs=[
            pl.BlockSpec((tm, d), lambda i, j: (i, 0)),
            pl.BlockSpec((1, N_MOD, d), lambda i, j: (i // per_seq, 0, 0)),
            pl.BlockSpec((1, d), lambda i, j: (0, 0)),
            w_spec,
            pl.BlockSpec((None, 1, tn), lambda i, j: (layer, 0, _src_col_block(j))),
        ],
        out_specs=pl.BlockSpec((tm, tn), lambda i, j: (i, j)),
        out_shape=jax.ShapeDtypeStruct((t, C_IN), BF16),
        scratch_shapes=[pltpu.VMEM((tm, d), BF16)],
        compiler_params=_params("arbitrary", "arbitrary"),
        name="norm_in_proj",
    )(x, mod, g.reshape(1, d), w, b)


def _lru_kernel(u_ref, gate_ref, mg_ref, convw_ref, convb_ref, wg_ref, bg_ref, lam_ref,
                wout_ref, wa32_ref, wo32_ref, o_ref, wa16_ref, wo16_ref,
                ubuf, a_s, b_s, y_s, hcarry, wout_bf, *, group, per_seq):
    wa16_ref[...] = wa32_ref[...].astype(BF16)
    wo16_ref[...] = wo32_ref[...].astype(BF16)
    ts = u_ref.shape[0]
    sub = ts // SUBLANES
    pitch = sub + 4
    halo = SUBLANES
    g_idx = pl.program_id(0)
    first = g_idx % per_seq == 0

    @pl.when(g_idx == 0)
    def _():
        ubuf[:, ts:ts + halo, :] = jnp.zeros((LRU_HEADS, halo, LANES), F32)
        hcarry[...] = jnp.zeros_like(hcarry)
        y_s[...] = jnp.zeros_like(y_s)
        wout_bf[...] = wout_ref[...].astype(BF16)

    ubuf[:, 0:halo, :] = jnp.where(first, 0.0, ubuf[:, ts:ts + halo, :])
    hcarry[...] = jnp.where(first, 0.0, hcarry[...])

    neg_c_softplus = -LRU_C * _softplus(-lam_ref[...])
    out_chunks = 8
    out_cols = D_MODEL // out_chunks
    heads_per_chunk = LRU_HEADS // out_chunks

    for hd in range(LRU_HEADS):
        if hd % heads_per_chunk == 0:
            c0 = (hd // heads_per_chunk) * out_cols
            out = jnp.dot(y_s[...], wout_bf[:, c0:c0 + out_cols], preferred_element_type=F32)
            o_ref[:, c0:c0 + out_cols] = (
                mg_ref[:, c0:c0 + out_cols].astype(F32) * out).astype(o_ref.dtype)
        cs = slice(hd * LRU_HEAD_DIM, (hd + 1) * LRU_HEAD_DIM)
        ubuf[hd, halo:halo + ts, :] = u_ref[:, cs].astype(F32)
        uc = convb_ref[:, cs]
        for k in range(CONV_W):
            off = halo - (CONV_W - 1) + k
            uc = uc + convw_ref[k:k + 1, cs] * ubuf[hd, off:off + ts, :]
        g = jnp.dot(uc.astype(BF16), wg_ref[hd], preferred_element_type=F32) + bg_ref[hd]
        r = jax.nn.sigmoid(g[:, :LRU_HEAD_DIM])
        i = jax.nn.sigmoid(g[:, LRU_HEAD_DIM:])
        log_a = r * neg_c_softplus[:, cs]
        a = jnp.exp(log_a)
        one_m_a2 = (1.0 + a * a) * jnp.tanh(-log_a)
        beta = jnp.where(one_m_a2 > 0.0, one_m_a2 * lax.rsqrt(one_m_a2), 0.0)
        b = beta * (i * uc)
        for s in range(SUBLANES):
            a_s[hd, s * pitch:s * pitch + sub, :] = a[s * sub:(s + 1) * sub, :]
            b_s[hd, s * pitch:s * pitch + sub, :] = b[s * sub:(s + 1) * sub, :]

    row = lax.broadcasted_iota(jnp.int32, (SUBLANES, LANES), 0)
    zeros = jnp.zeros((SUBLANES, LANES), F32)
    ones = jnp.ones((SUBLANES, LANES), F32)

    for h0 in range(0, LRU_HEADS, group):
        heads = range(h0, h0 + group)

        def strided(ref, hd, m):
            return ref[hd, pl.ds(m, SUBLANES, stride=pitch), :]

        def pass1(m, carry):
            out = []
            for n, hd in enumerate(heads):
                a_m = strided(a_s, hd, m)
                out += [a_m * carry[2 * n] + strided(b_s, hd, m), carry[2 * n + 1] * a_m]
            return tuple(out)

        ends = lax.fori_loop(0, sub, pass1, (zeros, ones) * group, unroll=2)

        starts = []
        for n, hd in enumerate(heads):
            e, q = ends[2 * n], ends[2 * n + 1]
            for d in (1, 2, 4):
                keep = row >= d
                e_prev = pltpu.roll(e, d, 0)
                q_prev = pltpu.roll(q, d, 0)
                e = jnp.where(keep, q * e_prev, 0.0) + e
                q = jnp.where(keep, q * q_prev, q)
            cs = slice(hd * LRU_HEAD_DIM, (hd + 1) * LRU_HEAD_DIM)
            h_in = hcarry[:, cs]
            end_state = e + q * h_in
            hcarry[:, cs] = end_state[SUBLANES - 1:SUBLANES, :]
            starts.append(jnp.where(row == 0, h_in, pltpu.roll(end_state, 1, 0)))

        def pass2(m, carry):
            out = []
            for n, hd in enumerate(heads):
                h = strided(a_s, hd, m) * carry[n] + strided(b_s, hd, m)
                b_s[hd, pl.ds(m, SUBLANES, stride=pitch), :] = h
                out.append(h)
            return tuple(out)

        lax.fori_loop(0, sub, pass2, tuple(starts), unroll=2)

    for hd in range(LRU_HEADS):
        cs = slice(hd * LRU_HEAD_DIM, (hd + 1) * LRU_HEAD_DIM)
        h = jnp.concatenate(
            [b_s[hd, s * pitch:s * pitch + sub, :] for s in range(SUBLANES)], axis=0)
        y_s[:, cs] = (h * gate_ref[:, cs].astype(F32)).astype(BF16)


def _lru_branch(proj, conv_w, conv_b, wg, bg, lam, w_out, w_attn, w_o, layer, batch, seq, ts):
    t = proj.shape[0]
    d = D_MODEL
    per_seq = seq // ts
    n_tiles = batch * per_seq
    pitch = ts // SUBLANES + 4
    cur = lambda g: jnp.minimum(g, n_tiles - 1)
    prev = lambda g: jnp.maximum(g - 1, 0)
    wa_in, wa_out, wa_shape = _cast_specs(w_attn.shape, layer, lambda g: g, n_tiles + 1)
    wo_in, wo_out, wo_shape = _cast_specs(w_o.shape, layer, lambda g: g, n_tiles + 1)
    return pl.pallas_call(
        functools.partial(_lru_kernel, group=8, per_seq=per_seq),
        grid=(n_tiles + 1,),
        in_specs=[
            pl.BlockSpec((ts, d), lambda g: (cur(g), OUT_U)),
            pl.BlockSpec((ts, d), lambda g: (cur(g), OUT_GATE)),
            pl.BlockSpec((ts, d), lambda g: (prev(g), OUT_GA)),
            _resident((CONV_W, d), lambda g: (0, 0)),
            _resident((1, d), lambda g: (0, 0)),
            _resident((LRU_HEADS, LRU_HEAD_DIM, 2 * LRU_HEAD_DIM), lambda g: (0, 0, 0)),
            _resident((LRU_HEADS, 1, 2 * LRU_HEAD_DIM), lambda g: (0, 0, 0)),
            _resident((1, d), lambda g: (0, 0)),
            _resident((None, d, d), lambda g: (layer, 0, 0)),
            wa_in,
            wo_in,
        ],
        out_specs=[pl.BlockSpec((ts, d), lambda g: (prev(g), 0)), wa_out, wo_out],
        out_shape=[jax.ShapeDtypeStruct((t, d), BF16), wa_shape, wo_shape],
        scratch_shapes=[
            pltpu.VMEM((LRU_HEADS, ts + SUBLANES, LANES), F32),
            pltpu.VMEM((LRU_HEADS, SUBLANES * pitch, LANES), F32),
            pltpu.VMEM((LRU_HEADS, SUBLANES * pitch, LANES), F32),
            pltpu.VMEM((ts, d), BF16),
            pltpu.VMEM((1, d), F32),
            pltpu.VMEM((d, d), BF16),
        ],
        compiler_params=_params("arbitrary"),
        name="rglru_branch",
    )(proj, proj, proj, conv_w, conv_b.reshape(1, d), wg, bg, lam.reshape(1, d), w_out,
      w_attn, w_o)


def _attn_kernel(sinks_ref, q_ref, k_ref, v_ref, kp_ref, vp_ref, ma_ref, gb_ref, x_ref,
                 mod_ref, wa_ref, wo_ref, wfi32_ref, wfo32_ref, o_ref, wfi16_ref, wfo16_ref, y_s):
    wfi16_ref[...] = wfi32_ref[...].astype(BF16)
    wfo16_ref[...] = wfo32_ref[...].astype(BF16)

    tq = q_ref.shape[0]
    n_blk = tq // WINDOW
    first = pl.program_id(1) == 0
    rows = GQA_GROUP * WINDOW

    qi = lax.broadcasted_iota(jnp.int32, (rows, WINDOW), 0) % WINDOW
    kk = lax.broadcasted_iota(jnp.int32, (rows, WINDOW), 1)
    in_cur = kk <= qi
    scale = HEAD_DIM ** -0.5
    nt_dims = (((1,), (1,)), ((), ()))
    ones_blk = jnp.ones((2 * WINDOW, HEAD_DIM), BF16)

    for h in range(N_KV_HEADS):
        hs = slice(h * HEAD_DIM, (h + 1) * HEAD_DIM)
        sink = jnp.concatenate(
            [jnp.full((WINDOW, 1), sinks_ref[h * GQA_GROUP + g], F32) for g in range(GQA_GROUP)],
            axis=0)
        for n in range(n_blk):
            rs = slice(n * WINDOW, (n + 1) * WINDOW)
            q4 = jnp.concatenate(
                [q_ref[rs, (h * GQA_GROUP + g) * HEAD_DIM:(h * GQA_GROUP + g + 1) * HEAD_DIM]
                 for g in range(GQA_GROUP)], axis=0)
            if n == 0:
                k_prev, v_prev = kp_ref[:, hs], vp_ref[:, hs]
            else:
                ps = slice((n - 1) * WINDOW, n * WINDOW)
                k_prev, v_prev = k_ref[ps, hs], v_ref[ps, hs]
            k_cat = jnp.concatenate([k_prev, k_ref[rs, hs]], axis=0)
            v_cat = jnp.concatenate([v_prev, v_ref[rs, hs]], axis=0)
            s = lax.dot_general(q4, k_cat, nt_dims, preferred_element_type=F32)
            s_prev = s[:, :WINDOW]
            if n == 0:
                s_prev = jnp.where(first, -jnp.inf, s_prev)
            sc = jnp.where(in_cur, s[:, WINDOW:], s_prev) * scale
            m = jnp.maximum(jnp.max(sc, axis=-1, keepdims=True), sink)
            p = jnp.exp(sc - m)
            p_cat = jnp.concatenate(
                [jnp.where(in_cur, 0.0, p), jnp.where(in_cur, p, 0.0)], axis=1).astype(BF16)
            o_ext = jnp.dot(p_cat, jnp.concatenate([v_cat, ones_blk], axis=1),
                            preferred_element_type=F32)
            denom = o_ext[:, HEAD_DIM:] + jnp.exp(sink - m)
            o = o_ext[:, :HEAD_DIM] / denom
            for g in range(GQA_GROUP):
                cs = slice((h * GQA_GROUP + g) * HEAD_DIM, (h * GQA_GROUP + g + 1) * HEAD_DIM)
                y_s[rs, cs] = o[g * WINDOW:(g + 1) * WINDOW].astype(BF16)

    attn = jnp.dot(y_s[...], wa_ref[...], preferred_element_type=F32)
    merged = gb_ref[...].astype(F32) * attn + ma_ref[...].astype(F32)
    r = jnp.dot(merged.astype(BF16), wo_ref[...], preferred_element_type=F32)
    o_ref[...] = x_ref[...] + mod_ref[0, 2:3, :] * r


def _attn_merge(proj, merged_a, x, sinks, mod, w_attn, w_o, w_ffn_in, w_ffn_out, layer,
                batch, seq, tq):
    t, d = x.shape
    per_seq = seq // tq
    n_blk = tq // WINDOW
    row = lambda b, s: b * per_seq + s
    prev_blk = lambda b, s: jnp.maximum(row(b, s) * n_blk - 1, 0)
    fi_in, fi_out, fi_shape = _cast_specs(w_ffn_in.shape, layer, row, batch * per_seq)
    fo_in, fo_out, fo_shape = _cast_specs(w_ffn_out.shape, layer, row, batch * per_seq)
    return pl.pallas_call(
        _attn_kernel,
        grid=(batch, per_seq),
        in_specs=[
            pl.BlockSpec(memory_space=pltpu.SMEM),
            pl.BlockSpec((tq, d), lambda b, s: (row(b, s), OUT_Q)),
            pl.BlockSpec((tq, KV_WIDTH), lambda b, s: (row(b, s), OUT_K)),
            pl.BlockSpec((tq, KV_WIDTH), lambda b, s: (row(b, s), OUT_V)),
            pl.BlockSpec((WINDOW, KV_WIDTH), lambda b, s: (prev_blk(b, s), OUT_K)),
            pl.BlockSpec((WINDOW, KV_WIDTH), lambda b, s: (prev_blk(b, s), OUT_V)),
            pl.BlockSpec((tq, d), lambda b, s: (row(b, s), 0)),
            pl.BlockSpec((tq, d), lambda b, s: (row(b, s), OUT_GB)),
            pl.BlockSpec((tq, d), lambda b, s: (row(b, s), 0)),
            pl.BlockSpec((1, N_MOD, d), lambda b, s: (b, 0, 0)),
            _resident((d, d), lambda b, s: (0, 0)),
            _resident((d, d), lambda b, s: (0, 0)),
            fi_in,
            fo_in,
        ],
        out_specs=[pl.BlockSpec((tq, d), lambda b, s: (row(b, s), 0)), fi_out, fo_out],
        out_shape=[jax.ShapeDtypeStruct((t, d), F32), fi_shape, fo_shape],
        scratch_shapes=[pltpu.VMEM((tq, d), BF16)],
        compiler_params=_params("arbitrary", "arbitrary"),
        name="swa_merge_out_proj",
    )(sinks, proj, proj, proj, proj, proj, merged_a, proj, x, mod, w_attn, w_o,
      w_ffn_in, w_ffn_out)


def _ffn_kernel(x_ref, mod_ref, g_ref, wg_ref, wu_ref, wo_ref, fg_ref, *rest, rows, final_norm,
                side_cast):
    if side_cast:
        w32_ref, o_ref, w16_ref, h_ref = rest
        w16_ref[...] = w32_ref[...].astype(BF16)
    else:
        o_ref, h_ref = rest
    acc_ref = o_ref
    tm, d = x_ref.shape
    th = wg_ref.shape[1]
    j = pl.program_id(1)

    @pl.when(j == 0)
    def _():
        _norm_mod_into(x_ref, h_ref, g_ref[...] * (1.0 + mod_ref[0, 4:5, :]), mod_ref[0, 3:4, :],
                       rows, also_zero=acc_ref)

    h = h_ref[...]
    acts = []
    for c0 in range(0, th, 256):
        gate = jnp.dot(h, wg_ref[:, c0:c0 + 256], preferred_element_type=F32)
        up = jnp.dot(h, wu_ref[:, c0:c0 + 256], preferred_element_type=F32)
        acts.append((gate * _sigmoid_tanh(gate) * up).astype(BF16))
    act = jnp.concatenate(acts, axis=1)
    for c0 in range(0, d, 512):
        acc_ref[:, c0:c0 + 512] += jnp.dot(act, wo_ref[:, c0:c0 + 512],
                                            preferred_element_type=F32)

    @pl.when(j == pl.num_programs(1) - 1)
    def _():
        def body(i, carry):
            r = pl.ds(pl.multiple_of(i * rows, rows), rows)
            y = x_ref[r, :] + mod_ref[0, 5:6, :] * acc_ref[r, :]
            if final_norm:
                inv = lax.rsqrt(jnp.mean(y * y, axis=-1, keepdims=True) + EPS)
                y = (y * inv) * fg_ref[...]
            o_ref[r, :] = y
            return carry
        lax.fori_loop(0, tm // rows, body, 0)


def _ffn(x, mod, g, w_in, w_out, final_g, final_norm, seq, tm, th, w_next=None, next_layer=None):
    t, d = x.shape
    hidden = w_out.shape[0]
    per_seq = seq // tm
    n_h = hidden // th
    n_tiles = t // tm
    in_specs = [
        pl.BlockSpec((tm, d), lambda i, j: (i, 0)),
        pl.BlockSpec((1, N_MOD, d), lambda i, j: (i // per_seq, 0, 0)),
        pl.BlockSpec((1, d), lambda i, j: (0, 0)),
        pl.BlockSpec((d, th), lambda i, j: (0, j)),
        pl.BlockSpec((d, th), lambda i, j: (0, j + n_h)),
        pl.BlockSpec((th, d), lambda i, j: (j, 0)),
        pl.BlockSpec((1, d), lambda i, j: (0, 0)),
    ]
    out_specs = [pl.BlockSpec((tm, d), lambda i, j: (i, 0))]
    out_shape = [jax.ShapeDtypeStruct((t, d), F32)]
    args = [x, mod, g.reshape(1, d), w_in, w_in, w_out, final_g.reshape(1, d)]
    side_cast = w_next is not None
    if side_cast:
        c_in, c_out, c_shape = _cast_specs(w_next.shape, next_layer, lambda i, j: i * n_h + j,
                                           n_tiles * n_h)
        in_specs.append(c_in)
        out_specs.append(c_out)
        out_shape.append(c_shape)
        args.append(w_next)
    res = pl.pallas_call(
        functools.partial(_ffn_kernel, rows=min(tm, 128), final_norm=final_norm,
                          side_cast=side_cast),
        grid=(n_tiles, n_h),
        in_specs=in_specs,
        out_specs=out_specs,
        out_shape=out_shape,
        scratch_shapes=[pltpu.VMEM((tm, d), BF16)],
        compiler_params=_params("arbitrary", "arbitrary"),
        name="norm_swiglu",
    )(*args)
    return (res[0], res[1]) if side_cast else (res[0], None)


def kernel(x, c, ada_w, ada_b, norm1_g, w_in, b_in, conv_w, conv_b, lru_wa, lru_ba, lru_wx,
           lru_bx, lru_lambda, sinks, w_lru_out, w_attn_out, w_o, norm2_g, w_ffn_in, w_ffn_out,
           final_g):
    batch, seq, d = x.shape
    assert d == D_MODEL and seq % WINDOW == 0
    depth = ada_w.shape[0]
    t = batch * seq
    tm_proj = min(seq, 1024)
    tm_ffn = min(seq, 1024)
    ts_lru = min(seq, 256)
    tq_attn = min(seq, 256)

    wg =jnp.concatenate([lru_wa, lru_wx], axis=-1).astype(BF16)
    bg = jnp.concatenate([lru_ba.reshape(depth, LRU_HEADS, 1, LRU_HEAD_DIM),
                          lru_bx.reshape(depth, LRU_HEADS, 1, LRU_HEAD_DIM)], axis=-1)
    b_in3 = b_in.reshape(depth, 1, C_IN)

    mod = _modulation(c, ada_w, ada_b)
    xf = x.reshape(t, d)
    w_in_l = w_in
    for l in range(depth):
        last = l == depth - 1
        proj = _in_proj(xf, mod[l], norm1_g[l], w_in_l, b_in3, l, seq, tm_proj)
        merged_a, w_attn_bf, w_o_bf = _lru_branch(
            proj, conv_w[l], conv_b[l], wg[l], bg[l], lru_lambda[l], w_lru_out, w_attn_out, w_o,
            l, batch, seq, ts_lru)
        xf, w_ffn_in_bf, w_ffn_out_bf = _attn_merge(
            proj, merged_a, xf, sinks[l], mod[l], w_attn_bf, w_o_bf, w_ffn_in, w_ffn_out, l,
            batch, seq, tq_attn)
        xf, w_in_l = _ffn(xf, mod[l], norm2_g[l], w_ffn_in_bf, w_ffn_out_bf, final_g, last, seq,
                          tm_ffn, 512, w_next=None if last else w_in,
                          next_layer=None if last else l + 1)
    return xf.reshape(batch, seq, d)
```

```python
import functools

import jax
import jax.numpy as jnp
from jax import lax
from jax.experimental import pallas as pl
from jax.experimental.pallas import tpu as pltpu

F32 = jnp.float32
BF16 = jnp.bfloat16

D_MODEL = 2048
LRU_HEADS = 16
LRU_HEAD_DIM = D_MODEL // LRU_HEADS
CONV_W = 4
LRU_C = 8.0
LOG2_E = 1.4426950408889634
HEAD_DIM = 128
N_Q_HEADS = 16
N_KV_HEADS = 4
GQA_GROUP = N_Q_HEADS // N_KV_HEADS
WINDOW = 128
KV_WIDTH = N_KV_HEADS * HEAD_DIM
N_MOD = 6
EPS = 1e-6
C_IN = 5 * D_MODEL + 2 * KV_WIDTH

PROJ_TN = 2 * KV_WIDTH
_BLK = D_MODEL // PROJ_TN
_KV_BLK = 2 * KV_WIDTH // PROJ_TN
OUT_U, OUT_GATE, OUT_Q, OUT_GA, OUT_GB = 0, 1, 2, 3, 4
OUT_K = 5 * D_MODEL // KV_WIDTH
OUT_V = OUT_K + 1
N_PROJ_BLK = C_IN // PROJ_TN

SUBLANES = 8
LANES = 128
VMEM_LIMIT = 56 * 1024 * 1024


def _src_col_block(j):
    return jnp.where(j < 3 * _BLK, j, jnp.where(j < 5 * _BLK, j + _KV_BLK, j - 2 * _BLK))


def _params(*sem):
    return pltpu.CompilerParams(dimension_semantics=sem, vmem_limit_bytes=VMEM_LIMIT)


def _cast_plan(rows, n_steps):
    k = n_steps
    while rows % k or (rows // k) % (2 * SUBLANES):
        k -= 1
    return rows // k, k


def _cast_specs(shape, layer, step_of, n_steps):
    rows, cols = shape[1:]
    blk, n_blk = _cast_plan(rows, n_steps)
    walk = lambda *g: jnp.minimum(step_of(*g), n_blk - 1)
    return (pl.BlockSpec((None, blk, cols), lambda *g: (layer, walk(*g), 0)),
            pl.BlockSpec((blk, cols), lambda *g: (walk(*g), 0)),
            jax.ShapeDtypeStruct((rows, cols), BF16))


def _resident(shape, index_map):
    return pl.BlockSpec(shape, index_map, pipeline_mode=pl.Buffered(1))


def _norm_mod_into(x_ref, h_ref, gain, shift, rows, also_zero=None):
    tm, d = x_ref.shape
    group = 2 * SUBLANES

    def body(i, carry):
        for q in range(rows // group):
            r = pl.ds(pl.multiple_of(i * rows + q * group, group), group)
            x = x_ref[r, :]
            inv = lax.rsqrt(jnp.mean(x * x, axis=-1, keepdims=True) + EPS)
            h_ref[r, :] = ((x * inv) * gain + shift).astype(BF16)
            if also_zero is not None:
                also_zero[r, :] = jnp.zeros((group, d), F32)
        return carry

    lax.fori_loop(0, tm // rows, body, 0)


_SQRT_2_OVER_PI = 0.7978845608028654


def _gelu_tanh(x):
    hx = 0.5 * x
    return hx + hx * jnp.tanh(x * (_SQRT_2_OVER_PI + (_SQRT_2_OVER_PI * 0.044715) * (x * x)))


def _sigmoid_tanh(x):
    return 0.5 * jnp.tanh(0.5 * x) + 0.5


def _softplus(z):
    return jnp.maximum(z, 0.0) + jnp.log1p(jnp.exp(-jnp.abs(z)))


def _mod_kernel(c_ref, w_ref, b_ref, o_ref):
    c = c_ref[...]
    act = (c * jax.nn.sigmoid(c)).astype(BF16)
    o_ref[0] = jnp.dot(act, w_ref[0].astype(BF16), preferred_element_type=F32) + b_ref[0]


def _modulation(c, ada_w, ada_b, tn=1024):
    depth, d, n = ada_w.shape
    b = c.shape[0]
    rows = -(-b // SUBLANES) * SUBLANES
    c_pad = jnp.zeros((rows, d), F32).at[:b].set(c)
    out = pl.pallas_call(
        _mod_kernel,
        grid=(depth, n // tn),
        in_specs=[
            pl.BlockSpec((rows, d), lambda l, j: (0, 0)),
            pl.BlockSpec((1, d, tn), lambda l, j: (l, 0, j)),
            pl.BlockSpec((1, 1, tn), lambda l, j: (l, 0, j)),
        ],
        out_specs=pl.BlockSpec((1, rows, tn), lambda l, j: (l, 0, j)),
        out_shape=jax.ShapeDtypeStruct((depth, rows, n), F32),
        compiler_params=_params("arbitrary", "arbitrary"),
        name="adaln_modulation",
    )(c_pad, ada_w, ada_b.reshape(depth, 1, n))
    return out[:, :b].reshape(depth, b, N_MOD, d)


def _inproj_kernel(x_ref, mod_ref, g_ref, w_ref, b_ref, o_ref, h_ref, *, rows):
    tm = x_ref.shape[0]
    j = pl.program_id(1)

    @pl.when(j == 0)
    def _():
        _norm_mod_into(x_ref, h_ref, g_ref[...] * (1.0 + mod_ref[0, 1:2, :]), mod_ref[0, 0:1, :],
                       rows)

    def project(act):
        z = jnp.dot(h_ref[...], w_ref[...].astype(BF16), preferred_element_type=F32) + b_ref[...]
        o_ref[...] = act(z).astype(o_ref.dtype)

    blk = j // _BLK
    is_gelu = blk == OUT_GATE
    is_sigmoid = jnp.logical_or(blk == OUT_GA, blk == OUT_GB)

    @pl.when(is_gelu)
    def _():
        project(_gelu_tanh)

    @pl.when(is_sigmoid)
    def _():
        project(_sigmoid_tanh)

    @pl.when(jnp.logical_not(jnp.logical_or(is_gelu, is_sigmoid)))
    def _():
        project(lambda z: z)


def _in_proj(x, mod, g, w, b, layer, seq, tm):
    t, d = x.shape
    tn = PROJ_TN
    per_seq = seq // tm
    if w.ndim == 3:
        w_spec = pl.BlockSpec((None, d, tn), lambda i, j: (layer, 0, _src_col_block(j)))
    else:
        w_spec = pl.BlockSpec((d, tn), lambda i, j: (0, _src_col_block(j)))
    return pl.pallas_call(
        functools.partial(_inproj_kernel, rows=min(tm, 128)),
        grid=(t // tm, N_PROJ_BLK),
        in_specs=[
            pl.BlockSpec((tm, d), lambda i, j: (i, 0)),
            pl.BlockSpec((1, N_MOD, d), lambda i, j: (i // per_seq, 0, 0)),
            pl.BlockSpec((1, d), lambda i, j: (0, 0)),
            w_spec,
            pl.BlockSpec((None, 1, tn), lambda i, j: (layer, 0, _src_col_block(j))),
        ],
        out_specs=pl.BlockSpec((tm, tn), lambda i, j: (i, j)),
        out_shape=jax.ShapeDtypeStruct((t, C_IN), BF16),
        scratch_shapes=[pltpu.VMEM((tm, d), BF16)],
        compiler_params=_params("arbitrary", "arbitrary"),
        name="norm_in_proj",
    )(x, mod, g.reshape(1, d), w, b)


def _lru_kernel(u_ref, gate_ref, mg_ref, convw_ref, convb_ref, wg_ref, lam_ref,
                wout_ref, wa32_ref, wo32_ref, o_ref, wa16_ref, wo16_ref,
                ubuf, a_s, b_s, y_s, hcarry, wout_bf, *, group, per_seq):
    wa16_ref[...] = wa32_ref[...].astype(BF16)
    wo16_ref[...] = wo32_ref[...].astype(BF16)
    ts = u_ref.shape[0]
    sub = ts // SUBLANES
    pitch = sub + 4
    halo = SUBLANES
    g_idx = pl.program_id(0)
    first = g_idx % per_seq == 0

    @pl.when(g_idx == 0)
    def _():
        ubuf[:, ts:ts + halo, :] = jnp.zeros((LRU_HEADS, halo, LANES), F32)
        hcarry[...] = jnp.zeros_like(hcarry)
        y_s[...] = jnp.zeros_like(y_s)
        wout_bf[...] = wout_ref[...].astype(BF16)

    ubuf[:, 0:halo, :] = jnp.where(first, 0.0, ubuf[:, ts:ts + halo, :])
    hcarry[...] = jnp.where(first, 0.0, hcarry[...])

    c_softplus = LRU_C * _softplus(-lam_ref[...])
    bias_cols = (lax.broadcasted_iota(jnp.int32, (ts, LRU_HEAD_DIM), 1) < 2).astype(BF16)
    out_chunks = 8
    out_cols = D_MODEL // out_chunks
    heads_per_chunk = LRU_HEADS // out_chunks

    for hd in range(LRU_HEADS):
        if hd % heads_per_chunk == 0:
            c0 = (hd // heads_per_chunk) * out_cols
            out = jnp.dot(y_s[...], wout_bf[:, c0:c0 + out_cols], preferred_element_type=F32)
            o_ref[:, c0:c0 + out_cols] = (
                mg_ref[:, c0:c0 + out_cols].astype(F32) * out).astype(o_ref.dtype)
        cs = slice(hd * LRU_HEAD_DIM, (hd + 1) * LRU_HEAD_DIM)
        ubuf[hd, halo:halo + ts, :] = u_ref[:, cs].astype(F32)
        uc = convb_ref[:, cs]
        for k in range(CONV_W):
            off = halo - (CONV_W - 1) + k
            uc = uc + convw_ref[k:k + 1, cs] * ubuf[hd, off:off + ts, :]
        g = jnp.dot(jnp.concatenate([uc.astype(BF16), bias_cols], axis=1), wg_ref[hd],
                    preferred_element_type=F32)
        r = _sigmoid_tanh(g[:, :LRU_HEAD_DIM])
        i = _sigmoid_tanh(g[:, LRU_HEAD_DIM:])
        a = jnp.exp2(r * (c_softplus[:, cs] * -LOG2_E))
        one_m_a2 = (1.0 + a * a) * jnp.tanh(r * c_softplus[:, cs])
        beta = jnp.where(one_m_a2 > 0.0, one_m_a2 * lax.rsqrt(one_m_a2), 0.0)
        b = beta * (i * uc)
        for s in range(SUBLANES):
            a_s[hd, s * pitch:s * pitch + sub, :] = a[s * sub:(s + 1) * sub, :]
            b_s[hd, s * pitch:s * pitch + sub, :] = b[s * sub:(s + 1) * sub, :]

    row = lax.broadcasted_iota(jnp.int32, (SUBLANES, LANES), 0)
    zeros = jnp.zeros((SUBLANES, LANES), F32)
    ones = jnp.ones((SUBLANES, LANES), F32)

    for h0 in range(0, LRU_HEADS, group):
        heads = range(h0, h0 + group)

        def strided(ref, hd, m):
            return ref[hd, pl.ds(m, SUBLANES, stride=pitch), :]

        def pass1(m, carry):
            out = []
            for n, hd in enumerate(heads):
                a_m = strided(a_s, hd, m)
                out += [a_m * carry[2 * n] + strided(b_s, hd, m), carry[2 * n + 1] * a_m]
            return tuple(out)

        ends = lax.fori_loop(0, sub, pass1, (zeros, ones) * group, unroll=2)

        starts = []
        for n, hd in enumerate(heads):
            e, q = ends[2 * n], ends[2 * n + 1]
            for d in (1, 2, 4):
                keep = row >= d
                e_prev = pltpu.roll(e, d, 0)
                q_prev = pltpu.roll(q, d, 0)
                e = jnp.where(keep, q * e_prev, 0.0) + e
                q = jnp.where(keep, q * q_prev, q)
            cs = slice(hd * LRU_HEAD_DIM, (hd + 1) * LRU_HEAD_DIM)
            h_in = hcarry[:, cs]
            end_state = e + q * h_in
            hcarry[:, cs] = end_state[SUBLANES - 1:SUBLANES, :]
            starts.append(jnp.where(row == 0, h_in, pltpu.roll(end_state, 1, 0)))

        def pass2(m, carry):
            out = []
            for n, hd in enumerate(heads):
                h = strided(a_s, hd, m) * carry[n] + strided(b_s, hd, m)
                b_s[hd, pl.ds(m, SUBLANES, stride=pitch), :] = h
                out.append(h)
            return tuple(out)

        lax.fori_loop(0, sub, pass2, tuple(starts), unroll=2)

    for hd in range(LRU_HEADS):
        cs = slice(hd * LRU_HEAD_DIM, (hd + 1) * LRU_HEAD_DIM)
        h = jnp.concatenate(
            [b_s[hd, s * pitch:s * pitch + sub, :] for s in range(SUBLANES)], axis=0)
        y_s[:, cs] = (h * gate_ref[:, cs].astype(F32)).astype(BF16)


def _lru_branch(proj, conv_w, conv_b, wg, lam, w_out, w_attn, w_o, layer, batch, seq, ts):
    t = proj.shape[0]
    d = D_MODEL
    per_seq = seq // ts
    n_tiles = batch * per_seq
    pitch = ts // SUBLANES + 4
    cur = lambda g: jnp.minimum(g, n_tiles - 1)
    prev = lambda g: jnp.maximum(g - 1, 0)
    wa_in, wa_out, wa_shape = _cast_specs(w_attn.shape, layer, lambda g: g, n_tiles + 1)
    wo_in, wo_out, wo_shape = _cast_specs(w_o.shape, layer, lambda g: g, n_tiles + 1)
    return pl.pallas_call(
        functools.partial(_lru_kernel, group=8, per_seq=per_seq),
        grid=(n_tiles + 1,),
        in_specs=[
            pl.BlockSpec((ts, d), lambda g: (cur(g), OUT_U)),
            pl.BlockSpec((ts, d), lambda g: (cur(g), OUT_GATE)),
            pl.BlockSpec((ts, d), lambda g: (prev(g), OUT_GA)),
            _resident((CONV_W, d), lambda g: (0, 0)),
            _resident((1, d), lambda g: (0, 0)),
            _resident((LRU_HEADS, 2 * LRU_HEAD_DIM, 2 * LRU_HEAD_DIM), lambda g: (0, 0, 0)),
            _resident((1, d), lambda g: (0, 0)),
            _resident((None, d, d), lambda g: (layer, 0, 0)),
            wa_in,
            wo_in,
        ],
        out_specs=[pl.BlockSpec((ts, d), lambda g: (prev(g), 0)), wa_out, wo_out],
        out_shape=[jax.ShapeDtypeStruct((t, d), BF16), wa_shape, wo_shape],
        scratch_shapes=[
            pltpu.VMEM((LRU_HEADS, ts + SUBLANES, LANES), F32),
            pltpu.VMEM((LRU_HEADS, SUBLANES * pitch, LANES), F32),
            pltpu.VMEM((LRU_HEADS, SUBLANES * pitch, LANES), F32),
            pltpu.VMEM((ts, d), BF16),
            pltpu.VMEM((1, d), F32),
            pltpu.VMEM((d, d), BF16),
        ],
        compiler_params=_params("arbitrary"),
        name="rglru_branch",
    )(proj, proj, proj, conv_w, conv_b.reshape(1, d), wg, lam.reshape(1, d), w_out,
      w_attn, w_o)


def _attn_kernel(sinks_ref, q_ref, k_ref, v_ref, kp_ref, vp_ref, ma_ref, gb_ref, x_ref,
                 mod_ref, wa_ref, wo_ref, wfi32_ref, wfo32_ref, o_ref, wfi16_ref, wfo16_ref, y_s):
    wfi16_ref[...] = wfi32_ref[...].astype(BF16)
    wfo16_ref[...] = wfo32_ref[...].astype(BF16)

    tq = q_ref.shape[0]
    n_blk = tq // WINDOW
    first = pl.program_id(1) == 0
    rows = GQA_GROUP * WINDOW

    qi = lax.broadcasted_iota(jnp.int32, (rows, WINDOW), 0) % WINDOW
    kk = lax.broadcasted_iota(jnp.int32, (rows, WINDOW), 1)
    in_cur = kk <= qi
    scale = HEAD_DIM ** -0.5
    nt_dims = (((1,), (1,)), ((), ()))
    ones_blk = jnp.ones((2 * WINDOW, HEAD_DIM), BF16)

    for h in range(N_KV_HEADS):
        hs = slice(h * HEAD_DIM, (h + 1) * HEAD_DIM)
        sink = jnp.concatenate(
            [jnp.full((WINDOW, 1), sinks_ref[h * GQA_GROUP + g], F32) for g in range(GQA_GROUP)],
            axis=0)
        for n in range(n_blk):
            rs = slice(n * WINDOW, (n + 1) * WINDOW)
            q4 = jnp.concatenate(
                [q_ref[rs, (h * GQA_GROUP + g) * HEAD_DIM:(h * GQA_GROUP + g + 1) * HEAD_DIM]
                 for g in range(GQA_GROUP)], axis=0)
            if n == 0:
                k_prev, v_prev = kp_ref[:, hs], vp_ref[:, hs]
            else:
                ps = slice((n - 1) * WINDOW, n * WINDOW)
                k_prev, v_prev = k_ref[ps, hs], v_ref[ps, hs]
            k_cat = jnp.concatenate([k_prev, k_ref[rs, hs]], axis=0)
            v_cat = jnp.concatenate([v_prev, v_ref[rs, hs]], axis=0)
            s = lax.dot_general(q4, k_cat, nt_dims, preferred_element_type=F32)
            s_prev = s[:, :WINDOW]
            if n == 0:
                s_prev = jnp.where(first, -jnp.inf, s_prev)
            sc = jnp.where(in_cur, s[:, WINDOW:], s_prev) * scale
            m = jnp.maximum(jnp.max(sc, axis=-1, keepdims=True), sink)
            p = jnp.exp(sc - m)
            p_cat = jnp.concatenate(
                [jnp.where(in_cur, 0.0, p), jnp.where(in_cur, p, 0.0)], axis=1).astype(BF16)
            o_ext = jnp.dot(p_cat, jnp.concatenate([v_cat, ones_blk], axis=1),
                            preferred_element_type=F32)
            denom = o_ext[:, HEAD_DIM:] + jnp.exp(sink - m)
            o = o_ext[:, :HEAD_DIM] / denom
            for g in range(GQA_GROUP):
                cs = slice((h * GQA_GROUP + g) * HEAD_DIM, (h * GQA_GROUP + g + 1) * HEAD_DIM)
                y_s[rs, cs] = o[g * WINDOW:(g + 1) * WINDOW].astype(BF16)

    attn = jnp.dot(y_s[...], wa_ref[...], preferred_element_type=F32)
    merged = gb_ref[...].astype(F32) * attn + ma_ref[...].astype(F32)
    r = jnp.dot(merged.astype(BF16), wo_ref[...], preferred_element_type=F32)
    o_ref[...] = x_ref[...] + mod_ref[0, 2:3, :] * r


def _attn_merge(proj, merged_a, x, sinks, mod, w_attn, w_o, w_ffn_in, w_ffn_out, layer,
                batch, seq, tq):
    t, d = x.shape
    per_seq = seq // tq
    n_blk = tq // WINDOW
    row = lambda b, s: b * per_seq + s
    prev_blk = lambda b, s: jnp.maximum(row(b, s) * n_blk - 1, 0)
    fi_in, fi_out, fi_shape = _cast_specs(w_ffn_in.shape, layer, row, batch * per_seq)
    fo_in, fo_out, fo_shape = _cast_specs(w_ffn_out.shape, layer, row, batch * per_seq)
    return pl.pallas_call(
        _attn_kernel,
        grid=(batch, per_seq),
        in_specs=[
            pl.BlockSpec(memory_space=pltpu.SMEM),
            pl.BlockSpec((tq, d), lambda b, s: (row(b, s), OUT_Q)),
            pl.BlockSpec((tq, KV_WIDTH), lambda b, s: (row(b, s), OUT_K)),
            pl.BlockSpec((tq, KV_WIDTH), lambda b, s: (row(b, s), OUT_V)),
            pl.BlockSpec((WINDOW, KV_WIDTH), lambda b, s: (prev_blk(b, s), OUT_K)),
            pl.BlockSpec((WINDOW, KV_WIDTH), lambda b, s: (prev_blk(b, s), OUT_V)),
            pl.BlockSpec((tq, d), lambda b, s: (row(b, s), 0)),
            pl.BlockSpec((tq, d), lambda b, s: (row(b, s), OUT_GB)),
            pl.BlockSpec((tq, d), lambda b, s: (row(b, s), 0)),
            pl.BlockSpec((1, N_MOD, d), lambda b, s: (b, 0, 0)),
            _resident((d, d), lambda b, s: (0, 0)),
            _resident((d, d), lambda b, s: (0, 0)),
            fi_in,
            fo_in,
        ],
        out_specs=[pl.BlockSpec((tq, d), lambda b, s: (row(b, s), 0)), fi_out, fo_out],
        out_shape=[jax.ShapeDtypeStruct((t, d), F32), fi_shape, fo_shape],
        scratch_shapes=[pltpu.VMEM((tq, d), BF16)],
        compiler_params=_params("arbitrary", "arbitrary"),
        name="swa_merge_out_proj",
    )(sinks, proj, proj, proj, proj, proj, merged_a, proj, x, mod, w_attn, w_o,
      w_ffn_in, w_ffn_out)


def _ffn_kernel(x_ref, mod_ref, g_ref, wg_ref, wu_ref, wo_ref, fg_ref, *rest, rows, final_norm,
                side_cast):
    if side_cast:
        w32_ref, o_ref, w16_ref, h_ref = rest
        w16_ref[...] = w32_ref[...].astype(BF16)
    else:
        o_ref, h_ref = rest
    acc_ref = o_ref
    tm, d = x_ref.shape
    th = wg_ref.shape[1]
    j = pl.program_id(1)

    @pl.when(j == 0)
    def _():
        _norm_mod_into(x_ref, h_ref, g_ref[...] * (1.0 + mod_ref[0, 4:5, :]), mod_ref[0, 3:4, :],
                       rows, also_zero=acc_ref)

    h = h_ref[...]
    acts = []
    for c0 in range(0, th, 256):
        gate = jnp.dot(h, wg_ref[:, c0:c0 + 256], preferred_element_type=F32)
        up = jnp.dot(h, wu_ref[:, c0:c0 + 256], preferred_element_type=F32)
        acts.append((gate * _sigmoid_tanh(gate) * up).astype(BF16))
    act = jnp.concatenate(acts, axis=1)
    for c0 in range(0, d, 512):
        acc_ref[:, c0:c0 + 512] += jnp.dot(act, wo_ref[:, c0:c0 + 512],
                                            preferred_element_type=F32)

    @pl.when(j == pl.num_programs(1) - 1)
    def _():
        def body(i, carry):
            r = pl.ds(pl.multiple_of(i * rows, rows), rows)
            y = x_ref[r, :] + mod_ref[0, 5:6, :] * acc_ref[r, :]
            if final_norm:
                inv = lax.rsqrt(jnp.mean(y * y, axis=-1, keepdims=True) + EPS)
                y = (y * inv) * fg_ref[...]
            o_ref[r, :] = y
            return carry
        lax.fori_loop(0, tm // rows, body, 0)


def _ffn(x, mod, g, w_in, w_out, final_g, final_norm, seq, tm, th, w_next=None, next_layer=None):
    t, d = x.shape
    hidden = w_out.shape[0]
    per_seq = seq // tm
    n_h = hidden // th
    n_tiles = t // tm
    in_specs = [
        pl.BlockSpec((tm, d), lambda i, j: (i, 0)),
        pl.BlockSpec((1, N_MOD, d), lambda i, j: (i // per_seq, 0, 0)),
        pl.BlockSpec((1, d), lambda i, j: (0, 0)),
        pl.BlockSpec((d, th), lambda i, j: (0, j)),
        pl.BlockSpec((d, th), lambda i, j: (0, j + n_h)),
        pl.BlockSpec((th, d), lambda i, j: (j, 0)),
        pl.BlockSpec((1, d), lambda i, j: (0, 0)),
    ]
    out_specs = [pl.BlockSpec((tm, d), lambda i, j: (i, 0))]
    out_shape = [jax.ShapeDtypeStruct((t, d), F32)]
    args = [x, mod, g.reshape(1, d), w_in, w_in, w_out, final_g.reshape(1, d)]
    side_cast = w_next is not None
    if side_cast:
        c_in, c_out, c_shape = _cast_specs(w_next.shape, next_layer, lambda i, j: i * n_h + j,
                                           n_tiles * n_h)
        in_specs.append(c_in)
        out_specs.append(c_out)
        out_shape.append(c_shape)
        args.append(w_next)
    res = pl.pallas_call(
        functools.partial(_ffn_kernel, rows=min(tm, 128), final_norm=final_norm,
                          side_cast=side_cast),
        grid=(n_tiles, n_h),
        in_specs=in_specs,
        out_specs=out_specs,
        out_shape=out_shape,
        scratch_shapes=[pltpu.VMEM((tm, d), BF16)],
        compiler_params=_params("arbitrary", "arbitrary"),
        name="norm_swiglu",
    )(*args)
    return (res[0], res[1]) if side_cast else (res[0], None)


def kernel(x, c, ada_w, ada_b, norm1_g, w_in, b_in, conv_w, conv_b, lru_wa, lru_ba, lru_wx,
           lru_bx, lru_lambda, sinks, w_lru_out, w_attn_out, w_o, norm2_g, w_ffn_in, w_ffn_out,
           final_g):
    batch, seq, d = x.shape
    assert d == D_MODEL and seq % WINDOW == 0
    depth = ada_w.shape[0]
    t = batch * seq
    tm_proj = min(seq, 1024)
    tm_ffn = min(seq, 1024)
    ts_lru = min(seq, 256)
    tq_attn = min(seq, 256)

    bg = jnp.concatenate([lru_ba.reshape(depth, LRU_HEADS, 1, LRU_HEAD_DIM),
                          lru_bx.reshape(depth, LRU_HEADS, 1, LRU_HEAD_DIM)], axis=-1)
    bg_hi = bg.astype(BF16)
    bg_lo = (bg - bg_hi.astype(F32)).astype(BF16)
    wg = jnp.concatenate(
        [jnp.concatenate([lru_wa, lru_wx], axis=-1).astype(BF16), bg_hi, bg_lo,
         jnp.zeros((depth, LRU_HEADS, LRU_HEAD_DIM - 2, 2 * LRU_HEAD_DIM), BF16)], axis=2)
    b_in3 = b_in.reshape(depth, 1, C_IN)

    mod = _modulation(c, ada_w, ada_b)
    xf = x.reshape(t, d)
    w_in_l = w_in
    for l in range(depth):
        last = l == depth - 1
        proj = _in_proj(xf, mod[l], norm1_g[l], w_in_l, b_in3, l, seq, tm_proj)
        merged_a, w_attn_bf, w_o_bf = _lru_branch(
            proj, conv_w[l], conv_b[l], wg[l], lru_lambda[l], w_lru_out, w_attn_out, w_o,
            l, batch, seq, ts_lru)
        xf, w_ffn_in_bf, w_ffn_out_bf = _attn_merge(
            proj, merged_a, xf, sinks[l], mod[l], w_attn_bf, w_o_bf, w_ffn_in, w_ffn_out, l,
            batch, seq, tq_attn)
        xf, w_in_l = _ffn(xf, mod[l], norm2_g[l], w_ffn_in_bf, w_ffn_out_bf, final_g, last, seq,
                          tm_ffn, 512, w_next=None if last else w_in,
                          next_layer=None if last else l + 1)
    return xf.reshape(batch, seq, d)
```

```python
import functools

import jax
import jax.numpy as jnp
from jax import lax
from jax.experimental import pallas as pl
from jax.experimental.pallas import tpu as pltpu

F32 = jnp.float32
BF16 = jnp.bfloat16

D_MODEL = 2048
LRU_HEADS = 16
LRU_HEAD_DIM = D_MODEL // LRU_HEADS
CONV_W = 4
LRU_C = 8.0
LOG2_E = 1.4426950408889634
HEAD_DIM = 128
N_Q_HEADS = 16
N_KV_HEADS = 4
GQA_GROUP = N_Q_HEADS // N_KV_HEADS
WINDOW = 128
KV_WIDTH = N_KV_HEADS * HEAD_DIM
N_MOD = 6
EPS = 1e-6
C_IN = 5 * D_MODEL + 2 * KV_WIDTH

PROJ_TN = 2 * KV_WIDTH
_BLK = D_MODEL // PROJ_TN
_KV_BLK = 2 * KV_WIDTH // PROJ_TN
OUT_U, OUT_GATE, OUT_Q, OUT_GA, OUT_GB = 0, 1, 2, 3, 4
OUT_K = 5 * D_MODEL // KV_WIDTH
OUT_V = OUT_K + 1
N_PROJ_BLK = C_IN // PROJ_TN

SUBLANES = 8
LANES = 128
MXU_WIDTH = 256
VMEM_LIMIT = 56 * 1024 * 1024
N_BIAS_ROWS = 2

TM_PROJ = 1024
TM_FFN = 1024
TS_LRU = 256
TQ_ATTN = 256
FFN_TH = 2 * MXU_WIDTH
ADALN_TN = 1024
SCAN_UNROLL = 8


def _src_col_block(j):
    return jnp.where(j < 3 * _BLK, j, jnp.where(j < 5 * _BLK, j + _KV_BLK, j - 2 * _BLK))


def _params(*sem):
    return pltpu.CompilerParams(dimension_semantics=sem, vmem_limit_bytes=VMEM_LIMIT)


def _cast_plan(rows, n_steps):
    k = n_steps
    while rows % k or (rows // k) % (2 * SUBLANES):
        k -= 1
    return rows // k, k


def _cast_specs(shape, layer, step_of, n_steps):
    rows, cols = shape[1:]
    blk, n_blk = _cast_plan(rows, n_steps)
    walk = lambda *g: jnp.minimum(step_of(*g), n_blk - 1)
    return (pl.BlockSpec((None, blk, cols), lambda *g: (layer, walk(*g), 0)),
            pl.BlockSpec((blk, cols), lambda *g: (walk(*g), 0)),
            jax.ShapeDtypeStruct((rows, cols), BF16))


def _resident(shape, index_map):
    return pl.BlockSpec(shape, index_map, pipeline_mode=pl.Buffered(1))


def _norm_mod_into(x_ref, h_ref, gain, shift, rows, also_zero=None):
    tm, d = x_ref.shape
    group = 2 * SUBLANES

    def body(i, carry):
        for q in range(rows // group):
            r = pl.ds(pl.multiple_of(i * rows + q * group, group), group)
            x = x_ref[r, :]
            inv = lax.rsqrt(jnp.mean(x * x, axis=-1, keepdims=True) + EPS)
            h_ref[r, :] = ((x * inv) * gain + shift).astype(BF16)
            if also_zero is not None:
                also_zero[r, :] = jnp.zeros((group, d), F32)
        return carry

    lax.fori_loop(0, tm // rows, body, 0)


_SQRT_2_OVER_PI = 0.7978845608028654


def _gelu_tanh(x):
    hx = 0.5 * x
    return hx + hx * jnp.tanh(x * (_SQRT_2_OVER_PI + (_SQRT_2_OVER_PI * 0.044715) * (x * x)))


def _sigmoid_tanh(x):
    return 0.5 * jnp.tanh(0.5 * x) + 0.5


def _softplus(z):
    return jnp.maximum(z, 0.0) + jnp.log1p(jnp.exp(-jnp.abs(z)))


def _mod_kernel(c_ref, w_ref, b_ref, o_ref):
    c = c_ref[...]
    act = (c * jax.nn.sigmoid(c)).astype(BF16)
    o_ref[0] = jnp.dot(act, w_ref[0].astype(BF16), preferred_element_type=F32) + b_ref[0]


def _modulation(c, ada_w, ada_b):
    depth, d, n = ada_w.shape
    tn = ADALN_TN
    b = c.shape[0]
    rows = -(-b // SUBLANES) * SUBLANES
    c_pad = jnp.zeros((rows, d), F32).at[:b].set(c)
    out = pl.pallas_call(
        _mod_kernel,
        grid=(depth, n // tn),
        in_specs=[
            pl.BlockSpec((rows, d), lambda l, j: (0, 0)),
            pl.BlockSpec((1, d, tn), lambda l, j: (l, 0, j)),
            pl.BlockSpec((1, 1, tn), lambda l, j: (l, 0, j)),
        ],
        out_specs=pl.BlockSpec((1, rows, tn), lambda l, j: (l, 0, j)),
        out_shape=jax.ShapeDtypeStruct((depth, rows, n), F32),
        compiler_params=_params("arbitrary", "arbitrary"),
        name="adaln_modulation",
    )(c_pad, ada_w, ada_b.reshape(depth, 1, n))
    return out[:, :b].reshape(depth, b, N_MOD, d)


def _inproj_kernel(x_ref, mod_ref, g_ref, w_ref, b_ref, o_ref, h_ref, *, rows):
    tm = x_ref.shape[0]
    j = pl.program_id(1)

    @pl.when(j == 0)
    def _():
        _norm_mod_into(x_ref, h_ref, g_ref[...] * (1.0 + mod_ref[0, 1:2, :]), mod_ref[0, 0:1, :],
                       rows)

    def project(act):
        z = jnp.dot(h_ref[...], w_ref[...].astype(BF16), preferred_element_type=F32) + b_ref[...]
        o_ref[...] = act(z).astype(o_ref.dtype)

    blk = j // _BLK
    is_gelu = blk == OUT_GATE
    is_sigmoid = jnp.logical_or(blk == OUT_GA, blk == OUT_GB)

    @pl.when(is_gelu)
    def _():
        project(_gelu_tanh)

    @pl.when(is_sigmoid)
    def _():
        project(_sigmoid_tanh)

    @pl.when(jnp.logical_not(jnp.logical_or(is_gelu, is_sigmoid)))
    def _():
        project(lambda z: z)


def _in_proj(x, mod, g, w, b, layer, seq, tm):
    t, d = x.shape
    tn = PROJ_TN
    per_seq = seq // tm
    if w.ndim == 3:
        w_spec = pl.BlockSpec((None, d, tn), lambda i, j: (layer, 0, _src_col_block(j)))
    else:
        w_spec = pl.BlockSpec((d, tn), lambda i, j: (0, _src_col_block(j)))
    return pl.pallas_call(
        functools.partial(_inproj_kernel, rows=min(tm, 128)),
        grid=(t // tm, N_PROJ_BLK),
        in_specs=[
            pl.BlockSpec((tm, d), lambda i, j: (i, 0)),
            pl.BlockSpec((1, N_MOD, d), lambda i, j: (i // per_seq, 0, 0)),
            pl.BlockSpec((1, d), lambda i, j: (0, 0)),
            w_spec,
            pl.BlockSpec((None, 1, tn), lambda i, j: (layer, 0, _src_col_block(j))),
        ],
        out_specs=pl.BlockSpec((tm, tn), lambda i, j: (i, j)),
        out_shape=jax.ShapeDtypeStruct((t, C_IN), BF16),
        scratch_shapes=[pltpu.VMEM((tm, d), BF16)],
        compiler_params=_params("arbitrary", "arbitrary"),
        name="norm_in_proj",
    )(x, mod, g.reshape(1, d), w, b)


def _lru_kernel(u_ref, gate_ref, mg_ref, convw_ref, convb_ref, wg_ref, lam_ref,
                wout_ref, wa32_ref, wo32_ref, o_ref, wa16_ref, wo16_ref,
                ubuf, a_s, b_s, y_s, hcarry, wout_bf, *, group, per_seq):
    wa16_ref[...] = wa32_ref[...].astype(BF16)
    wo16_ref[...] = wo32_ref[...].astype(BF16)
    ts = u_ref.shape[0]
    sub = ts // SUBLANES
    pitch = sub + 4
    halo = SUBLANES
    g_idx = pl.program_id(0)
    first = g_idx % per_seq == 0

    @pl.when(g_idx == 0)
    def _():
        ubuf[:, ts:ts + halo, :] = jnp.zeros((LRU_HEADS, halo, LANES), F32)
        hcarry[...] = jnp.zeros_like(hcarry)
        y_s[...] = jnp.zeros_like(y_s)
        wout_bf[...] = wout_ref[...].astype(BF16)

    ubuf[:, 0:halo, :] = jnp.where(first, 0.0, ubuf[:, ts:ts + halo, :])
    hcarry[...] = jnp.where(first, 0.0, hcarry[...])

    c_softplus = LRU_C * _softplus(-lam_ref[...])
    bias_cols = (lax.broadcasted_iota(jnp.int32, (ts, LRU_HEAD_DIM), 1)
                 < N_BIAS_ROWS).astype(BF16)
    out_cols = MXU_WIDTH
    heads_per_chunk = out_cols // LRU_HEAD_DIM

    for hd in range(LRU_HEADS):
        if hd % heads_per_chunk == 0:
            c0 = (hd // heads_per_chunk) * out_cols
            out = jnp.dot(y_s[...], wout_bf[:, c0:c0 + out_cols], preferred_element_type=F32)
            o_ref[:, c0:c0 + out_cols] = (
                mg_ref[:, c0:c0 + out_cols].astype(F32) * out).astype(o_ref.dtype)
        cs = slice(hd * LRU_HEAD_DIM, (hd + 1) * LRU_HEAD_DIM)
        ubuf[hd, halo:halo + ts, :] = u_ref[:, cs].astype(F32)
        uc = convb_ref[:, cs]
        for k in range(CONV_W):
            off = halo - (CONV_W - 1) + k
            uc = uc + convw_ref[k:k + 1, cs] * ubuf[hd, off:off + ts, :]
        g = jnp.dot(jnp.concatenate([uc.astype(BF16), bias_cols], axis=1), wg_ref[hd],
                    preferred_element_type=F32)
        r = _sigmoid_tanh(g[:, :LRU_HEAD_DIM])
        i = _sigmoid_tanh(g[:, LRU_HEAD_DIM:])
        a = jnp.exp2(r * (c_softplus[:, cs] * -LOG2_E))
        one_m_a2 = (1.0 + a * a) * jnp.tanh(r * c_softplus[:, cs])
        beta = jnp.where(one_m_a2 > 0.0, one_m_a2 * lax.rsqrt(one_m_a2), 0.0)
        b = beta * (i * uc)
        for s in range(SUBLANES):
            a_s[hd, s * pitch:s * pitch + sub, :] = a[s * sub:(s + 1) * sub, :]
            b_s[hd, s * pitch:s * pitch + sub, :] = b[s * sub:(s + 1) * sub, :]

    row = lax.broadcasted_iota(jnp.int32, (SUBLANES, LANES), 0)
    zeros = jnp.zeros((SUBLANES, LANES), F32)
    ones = jnp.ones((SUBLANES, LANES), F32)

    for h0 in range(0, LRU_HEADS, group):
        heads = range(h0, h0 + group)

        def strided(ref, hd, m):
            return ref[hd, pl.ds(m, SUBLANES, stride=pitch), :]

        def pass1(m, carry):
            out = []
            for n, hd in enumerate(heads):
                a_m = strided(a_s, hd, m)
                out += [a_m * carry[2 * n] + strided(b_s, hd, m), carry[2 * n + 1] * a_m]
            return tuple(out)

        ends = lax.fori_loop(0, sub, pass1, (zeros, ones) * group, unroll=SCAN_UNROLL)

        starts = []
        for n, hd in enumerate(heads):
            e, q = ends[2 * n], ends[2 * n + 1]
            for d in (1, 2, 4):
                keep = row >= d
                e_prev = pltpu.roll(e, d, 0)
                q_prev = pltpu.roll(q, d, 0)
                e = jnp.where(keep, q * e_prev, 0.0) + e
                q = jnp.where(keep, q * q_prev, q)
            cs = slice(hd * LRU_HEAD_DIM, (hd + 1) * LRU_HEAD_DIM)
            h_in = hcarry[:, cs]
            end_state = e + q * h_in
            hcarry[:, cs] = end_state[SUBLANES - 1:SUBLANES, :]
            starts.append(jnp.where(row == 0, h_in, pltpu.roll(end_state, 1, 0)))

        def pass2(m, carry):
            out = []
            for n, hd in enumerate(heads):
                h = strided(a_s, hd, m) * carry[n] + strided(b_s, hd, m)
                b_s[hd, pl.ds(m, SUBLANES, stride=pitch), :] = h
                out.append(h)
            return tuple(out)

        lax.fori_loop(0, sub, pass2, tuple(starts), unroll=SCAN_UNROLL)

    for hd in range(LRU_HEADS):
        cs = slice(hd * LRU_HEAD_DIM, (hd + 1) * LRU_HEAD_DIM)
        h = jnp.concatenate(
            [b_s[hd, s * pitch:s * pitch + sub, :] for s in range(SUBLANES)], axis=0)
        y_s[:, cs] = (h * gate_ref[:, cs].astype(F32)).astype(BF16)


def _lru_branch(proj, conv_w, conv_b, wg, lam, w_out, w_attn, w_o, layer, batch, seq, ts):
    t = proj.shape[0]
    d = D_MODEL
    per_seq = seq // ts
    n_tiles = batch * per_seq
    pitch = ts // SUBLANES + 4
    cur = lambda g: jnp.minimum(g, n_tiles - 1)
    prev = lambda g: jnp.maximum(g - 1, 0)
    wa_in, wa_out, wa_shape = _cast_specs(w_attn.shape, layer, lambda g: g, n_tiles + 1)
    wo_in, wo_out, wo_shape = _cast_specs(w_o.shape, layer, lambda g: g, n_tiles + 1)
    return pl.pallas_call(
        functools.partial(_lru_kernel, group=8, per_seq=per_seq),
        grid=(n_tiles + 1,),
        in_specs=[
            pl.BlockSpec((ts, d), lambda g: (cur(g), OUT_U)),
            pl.BlockSpec((ts, d), lambda g: (cur(g), OUT_GATE)),
            pl.BlockSpec((ts, d), lambda g: (prev(g), OUT_GA)),
            _resident((CONV_W, d), lambda g: (0, 0)),
            _resident((1, d), lambda g: (0, 0)),
            _resident((LRU_HEADS, 2 * LRU_HEAD_DIM, 2 * LRU_HEAD_DIM), lambda g: (0, 0, 0)),
            _resident((1, d), lambda g: (0, 0)),
            _resident((None, d, d), lambda g: (layer, 0, 0)),
            wa_in,
            wo_in,
        ],
        out_specs=[pl.BlockSpec((ts, d), lambda g: (prev(g), 0)), wa_out, wo_out],
        out_shape=[jax.ShapeDtypeStruct((t, d), BF16), wa_shape, wo_shape],
        scratch_shapes=[
            pltpu.VMEM((LRU_HEADS, ts + SUBLANES, LANES), F32),
            pltpu.VMEM((LRU_HEADS, SUBLANES * pitch, LANES), F32),
            pltpu.VMEM((LRU_HEADS, SUBLANES * pitch, LANES), F32),
            pltpu.VMEM((ts, d), BF16),
            pltpu.VMEM((1, d), F32),
            pltpu.VMEM((d, d), BF16),
        ],
        compiler_params=_params("arbitrary"),
        name="rglru_branch",
    )(proj, proj, proj, conv_w, conv_b.reshape(1, d), wg, lam.reshape(1, d), w_out,
      w_attn, w_o)


def _attn_kernel(sinks_ref, q_ref, k_ref, v_ref, kp_ref, vp_ref, ma_ref, gb_ref, x_ref,
                 mod_ref, wa_ref, wo_ref, wfi32_ref, wfo32_ref, o_ref, wfi16_ref, wfo16_ref, y_s):
    wfi16_ref[...] = wfi32_ref[...].astype(BF16)
    wfo16_ref[...] = wfo32_ref[...].astype(BF16)

    tq = q_ref.shape[0]
    n_blk = tq // WINDOW
    first = pl.program_id(1) == 0
    rows = GQA_GROUP * WINDOW

    qi = lax.broadcasted_iota(jnp.int32, (rows, WINDOW), 0) % WINDOW
    kk = lax.broadcasted_iota(jnp.int32, (rows, WINDOW), 1)
    in_cur = kk <= qi
    scale = HEAD_DIM ** -0.5
    nt_dims = (((1,), (1,)), ((), ()))
    ones_blk = jnp.ones((2 * WINDOW, HEAD_DIM), BF16)

    for h in range(N_KV_HEADS):
        hs = slice(h * HEAD_DIM, (h + 1) * HEAD_DIM)
        sink = jnp.concatenate(
            [jnp.full((WINDOW, 1), sinks_ref[h * GQA_GROUP + g], F32) for g in range(GQA_GROUP)],
            axis=0)
        for n in range(n_blk):
            rs = slice(n * WINDOW, (n + 1) * WINDOW)
            q4 = jnp.concatenate(
                [q_ref[rs, (h * GQA_GROUP + g) * HEAD_DIM:(h * GQA_GROUP + g + 1) * HEAD_DIM]
                 for g in range(GQA_GROUP)], axis=0)
            if n == 0:
                k_prev, v_prev = kp_ref[:, hs], vp_ref[:, hs]
            else:
                ps = slice((n - 1) * WINDOW, n * WINDOW)
                k_prev, v_prev = k_ref[ps, hs], v_ref[ps, hs]
            k_cat = jnp.concatenate([k_prev, k_ref[rs, hs]], axis=0)
            v_cat = jnp.concatenate([v_prev, v_ref[rs, hs]], axis=0)
            s = lax.dot_general(q4, k_cat, nt_dims, preferred_element_type=F32)
            s_prev = s[:, :WINDOW]
            if n == 0:
                s_prev = jnp.where(first, -jnp.inf, s_prev)
            sc = jnp.where(in_cur, s[:, WINDOW:], s_prev) * scale
            m = jnp.maximum(jnp.max(sc, axis=-1, keepdims=True), sink)
            p = jnp.exp(sc - m)
            p_cat = jnp.concatenate(
                [jnp.where(in_cur, 0.0, p), jnp.where(in_cur, p, 0.0)], axis=1).astype(BF16)
            o_ext = jnp.dot(p_cat, jnp.concatenate([v_cat, ones_blk], axis=1),
                            preferred_element_type=F32)
            denom = o_ext[:, HEAD_DIM:] + jnp.exp(sink - m)
            o = o_ext[:, :HEAD_DIM] / denom
            for g in range(GQA_GROUP):
                cs = slice((h * GQA_GROUP + g) * HEAD_DIM, (h * GQA_GROUP + g + 1) * HEAD_DIM)
                y_s[rs, cs] = o[g * WINDOW:(g + 1) * WINDOW].astype(BF16)

    attn = jnp.dot(y_s[...], wa_ref[...], preferred_element_type=F32)
    merged = gb_ref[...].astype(F32) * attn + ma_ref[...].astype(F32)
    r = jnp.dot(merged.astype(BF16), wo_ref[...], preferred_element_type=F32)
    o_ref[...] = x_ref[...] + mod_ref[0, 2:3, :] * r


def _attn_merge(proj, merged_a, x, sinks, mod, w_attn, w_o, w_ffn_in, w_ffn_out, layer,
                batch, seq, tq):
    t, d = x.shape
    per_seq = seq // tq
    n_blk = tq // WINDOW
    row = lambda b, s: b * per_seq + s
    prev_blk = lambda b, s: jnp.maximum(row(b, s) * n_blk - 1, 0)
    fi_in, fi_out, fi_shape = _cast_specs(w_ffn_in.shape, layer, row, batch * per_seq)
    fo_in, fo_out, fo_shape = _cast_specs(w_ffn_out.shape, layer, row, batch * per_seq)
    return pl.pallas_call(
        _attn_kernel,
        grid=(batch, per_seq),
        in_specs=[
            pl.BlockSpec(memory_space=pltpu.SMEM),
            pl.BlockSpec((tq, d), lambda b, s: (row(b, s), OUT_Q)),
            pl.BlockSpec((tq, KV_WIDTH), lambda b, s: (row(b, s), OUT_K)),
            pl.BlockSpec((tq, KV_WIDTH), lambda b, s: (row(b, s), OUT_V)),
            pl.BlockSpec((WINDOW, KV_WIDTH), lambda b, s: (prev_blk(b, s), OUT_K)),
            pl.BlockSpec((WINDOW, KV_WIDTH), lambda b, s: (prev_blk(b, s), OUT_V)),
            pl.BlockSpec((tq, d), lambda b, s: (row(b, s), 0)),
            pl.BlockSpec((tq, d), lambda b, s: (row(b, s), OUT_GB)),
            pl.BlockSpec((tq, d), lambda b, s: (row(b, s), 0)),
            pl.BlockSpec((1, N_MOD, d), lambda b, s: (b, 0, 0)),
            _resident((d, d), lambda b, s: (0, 0)),
            _resident((d, d), lambda b, s: (0, 0)),
            fi_in,
            fo_in,
        ],
        out_specs=[pl.BlockSpec((tq, d), lambda b, s: (row(b, s), 0)), fi_out, fo_out],
        out_shape=[jax.ShapeDtypeStruct((t, d), F32), fi_shape, fo_shape],
        scratch_shapes=[pltpu.VMEM((tq, d), BF16)],
        compiler_params=_params("arbitrary", "arbitrary"),
        name="swa_merge_out_proj",
    )(sinks, proj, proj, proj, proj, proj, merged_a, proj, x, mod, w_attn, w_o,
      w_ffn_in, w_ffn_out)


def _ffn_kernel(x_ref, mod_ref, g_ref, wg_ref, wu_ref, wo_ref, fg_ref, *rest, rows, final_norm,
                side_cast):
    if side_cast:
        w32_ref, o_ref, w16_ref, h_ref = rest
        w16_ref[...] = w32_ref[...].astype(BF16)
    else:
        o_ref, h_ref = rest
    acc_ref = o_ref
    tm, d = x_ref.shape
    th = wg_ref.shape[1]
    j = pl.program_id(1)

    @pl.when(j == 0)
    def _():
        _norm_mod_into(x_ref, h_ref, g_ref[...] * (1.0 + mod_ref[0, 4:5, :]), mod_ref[0, 3:4, :],
                       rows, also_zero=acc_ref)

    h = h_ref[...]
    acts = []
    for c0 in range(0, th, MXU_WIDTH):
        gate = jnp.dot(h, wg_ref[:, c0:c0 + MXU_WIDTH], preferred_element_type=F32)
        up = jnp.dot(h, wu_ref[:, c0:c0 + MXU_WIDTH], preferred_element_type=F32)
        acts.append((gate * _sigmoid_tanh(gate) * up).astype(BF16))
    act = jnp.concatenate(acts, axis=1)
    oc = 2 * MXU_WIDTH
    for c0 in range(0, d, oc):
        acc_ref[:, c0:c0 + oc] += jnp.dot(act, wo_ref[:, c0:c0 + oc],
                                           preferred_element_type=F32)

    @pl.when(j == pl.num_programs(1) - 1)
    def _():
        def body(i, carry):
            r = pl.ds(pl.multiple_of(i * rows, rows), rows)
            y = x_ref[r, :] + mod_ref[0, 5:6, :] * acc_ref[r, :]
            if final_norm:
                inv = lax.rsqrt(jnp.mean(y * y, axis=-1, keepdims=True) + EPS)
                y = (y * inv) * fg_ref[...]
            o_ref[r, :] = y
            return carry
        lax.fori_loop(0, tm // rows, body, 0)


def _ffn(x, mod, g, w_in, w_out, final_g, final_norm, seq, tm, th, w_next=None, next_layer=None):
    t, d = x.shape
    hidden = w_out.shape[0]
    per_seq = seq // tm
    n_h = hidden // th
    n_tiles = t // tm
    in_specs = [
        pl.BlockSpec((tm, d), lambda i, j: (i, 0)),
        pl.BlockSpec((1, N_MOD, d), lambda i, j: (i // per_seq, 0, 0)),
        pl.BlockSpec((1, d), lambda i, j: (0, 0)),
        pl.BlockSpec((d, th), lambda i, j: (0, j)),
        pl.BlockSpec((d, th), lambda i, j: (0, j + n_h)),
        pl.BlockSpec((th, d), lambda i, j: (j, 0)),
        pl.BlockSpec((1, d), lambda i, j: (0, 0)),
    ]
    out_specs = [pl.BlockSpec((tm, d), lambda i, j: (i, 0))]
    out_shape = [jax.ShapeDtypeStruct((t, d), F32)]
    args = [x, mod, g.reshape(1, d), w_in, w_in, w_out, final_g.reshape(1, d)]
    side_cast = w_next is not None
    if side_cast:
        c_in, c_out, c_shape = _cast_specs(w_next.shape, next_layer, lambda i, j: i * n_h + j,
                                           n_tiles * n_h)
        in_specs.append(c_in)
        out_specs.append(c_out)
        out_shape.append(c_shape)
        args.append(w_next)
    res = pl.pallas_call(
        functools.partial(_ffn_kernel, rows=min(tm, 128), final_norm=final_norm,
                          side_cast=side_cast),
        grid=(n_tiles, n_h),
        in_specs=in_specs,
        out_specs=out_specs,
        out_shape=out_shape,
        scratch_shapes=[pltpu.VMEM((tm, d), BF16)],
        compiler_params=_params("arbitrary", "arbitrary"),
        name="norm_swiglu",
    )(*args)
    return (res[0], res[1]) if side_cast else (res[0], None)


def kernel(x, c, ada_w, ada_b, norm1_g, w_in, b_in, conv_w, conv_b, lru_wa, lru_ba, lru_wx,
           lru_bx, lru_lambda, sinks, w_lru_out, w_attn_out, w_o, norm2_g, w_ffn_in, w_ffn_out,
           final_g):
    batch, seq, d = x.shape
    assert d == D_MODEL and seq % WINDOW == 0
    depth = ada_w.shape[0]
    t = batch * seq
    tm_proj = min(seq, TM_PROJ)
    tm_ffn = min(seq, TM_FFN)
    ts_lru = min(seq, TS_LRU)
    tq_attn = min(seq, TQ_ATTN)

    bg = jnp.concatenate([lru_ba.reshape(depth, LRU_HEADS, 1, LRU_HEAD_DIM),
                          lru_bx.reshape(depth, LRU_HEADS, 1, LRU_HEAD_DIM)], axis=-1)
    bg_hi = bg.astype(BF16)
    bg_lo = (bg - bg_hi.astype(F32)).astype(BF16)
    wg = jnp.concatenate(
        [jnp.concatenate([lru_wa, lru_wx], axis=-1).astype(BF16), bg_hi, bg_lo,
         jnp.zeros((depth, LRU_HEADS, LRU_HEAD_DIM - N_BIAS_ROWS, 2 * LRU_HEAD_DIM), BF16)],
        axis=2)
    b_in3 = b_in.reshape(depth, 1, C_IN)

    mod = _modulation(c, ada_w, ada_b)
    xf = x.reshape(t, d)
    w_in_l = w_in
    for l in range(depth):
        last = l == depth - 1
        proj = _in_proj(xf, mod[l], norm1_g[l], w_in_l, b_in3, l, seq, tm_proj)
        merged_a, w_attn_bf, w_o_bf = _lru_branch(
            proj, conv_w[l], conv_b[l], wg[l], lru_lambda[l], w_lru_out, w_attn_out, w_o,
            l, batch, seq, ts_lru)
        xf, w_ffn_in_bf, w_ffn_out_bf = _attn_merge(
            proj, merged_a, xf, sinks[l], mod[l], w_attn_bf, w_o_bf, w_ffn_in, w_ffn_out, l,
            batch, seq, tq_attn)
        xf, w_in_l = _ffn(xf, mod[l], norm2_g[l], w_ffn_in_bf, w_ffn_out_bf, final_g, last, seq,
                          tm_ffn, FFN_TH, w_next=None if last else w_in,
                          next_layer=None if last else l + 1)
    return xf.reshape(batch, seq, d)
```

```python
import functools

import jax
import jax.numpy as jnp
from jax import lax
from jax.experimental import pallas as pl
from jax.experimental.pallas import tpu as pltpu

F32 = jnp.float32
BF16 = jnp.bfloat16

D_MODEL = 2048
LRU_HEADS = 16
LRU_HEAD_DIM = D_MODEL // LRU_HEADS
CONV_W = 4
LRU_C = 8.0
LOG2_E = 1.4426950408889634
HEAD_DIM = 128
N_Q_HEADS = 16
N_KV_HEADS = 4
GQA_GROUP = N_Q_HEADS // N_KV_HEADS
WINDOW = 128
KV_WIDTH = N_KV_HEADS * HEAD_DIM
N_MOD = 6
EPS = 1e-6
C_IN = 5 * D_MODEL + 2 * KV_WIDTH

PROJ_TN = 2 * KV_WIDTH
_BLK = D_MODEL // PROJ_TN
_KV_BLK = 2 * KV_WIDTH // PROJ_TN
OUT_U, OUT_GATE, OUT_Q, OUT_GA, OUT_GB = 0, 1, 2, 3, 4
OUT_KV = 5 * D_MODEL // (2 * KV_WIDTH)
N_PROJ_BLK = C_IN // PROJ_TN

SUBLANES = 8
LANES = 128
MXU_WIDTH = 256
VMEM_LIMIT = 56 * 1024 * 1024
N_BIAS_ROWS = 2

TM_PROJ = 1024
TM_FFN = 1024
TS_LRU = 256
TQ_ATTN = 256
FFN_TH = 2 * MXU_WIDTH
ADALN_TN = 1024
SCAN_UNROLL = 8


def _src_col_block(j):
    return jnp.where(j < 3 * _BLK, j, jnp.where(j < 5 * _BLK, j + _KV_BLK, j - 2 * _BLK))


def _params(*sem):
    return pltpu.CompilerParams(dimension_semantics=sem, vmem_limit_bytes=VMEM_LIMIT)


def _cast_plan(rows, n_steps):
    k = n_steps
    while rows % k or (rows // k) % (2 * SUBLANES):
        k -= 1
    return rows // k, k


def _cast_specs(shape, layer, step_of, n_steps):
    rows, cols = shape[1:]
    blk, n_blk = _cast_plan(rows, n_steps)
    walk = lambda *g: jnp.minimum(step_of(*g), n_blk - 1)
    return (pl.BlockSpec((None, blk, cols), lambda *g: (layer, walk(*g), 0)),
            pl.BlockSpec((blk, cols), lambda *g: (walk(*g), 0)),
            jax.ShapeDtypeStruct((rows, cols), BF16))


def _resident(shape, index_map):
    return pl.BlockSpec(shape, index_map, pipeline_mode=pl.Buffered(1))


def _norm_mod_into(x_ref, h_ref, gain, shift, rows, also_zero=None):
    tm, d = x_ref.shape
    group = 2 * SUBLANES

    def body(i, carry):
        for q in range(rows // group):
            r = pl.ds(pl.multiple_of(i * rows + q * group, group), group)
            x = x_ref[r, :]
            inv = lax.rsqrt(jnp.mean(x * x, axis=-1, keepdims=True) + EPS)
            h_ref[r, :] = ((x * inv) * gain + shift).astype(BF16)
            if also_zero is not None:
                also_zero[r, :] = jnp.zeros((group, d), F32)
        return carry

    lax.fori_loop(0, tm // rows, body, 0)


_SQRT_2_OVER_PI = 0.7978845608028654


def _gelu_tanh(x):
    hx = 0.5 * x
    return hx + hx * jnp.tanh(x * (_SQRT_2_OVER_PI + (_SQRT_2_OVER_PI * 0.044715) * (x * x)))


def _sigmoid_tanh(x):
    return 0.5 * jnp.tanh(0.5 * x) + 0.5


def _softplus(z):
    return jnp.maximum(z, 0.0) + jnp.log1p(jnp.exp(-jnp.abs(z)))


def _mod_kernel(c_ref, w_ref, b_ref, o_ref):
    c = c_ref[...]
    act = (c * jax.nn.sigmoid(c)).astype(BF16)
    o_ref[0] = jnp.dot(act, w_ref[0].astype(BF16), preferred_element_type=F32) + b_ref[0]


def _modulation(c, ada_w, ada_b):
    depth, d, n = ada_w.shape
    tn = ADALN_TN
    b = c.shape[0]
    rows = -(-b // SUBLANES) * SUBLANES
    c_pad = jnp.zeros((rows, d), F32).at[:b].set(c)
    out = pl.pallas_call(
        _mod_kernel,
        grid=(depth, n // tn),
        in_specs=[
            pl.BlockSpec((rows, d), lambda l, j: (0, 0)),
            pl.BlockSpec((1, d, tn), lambda l, j: (l, 0, j)),
            pl.BlockSpec((1, 1, tn), lambda l, j: (l, 0, j)),
        ],
        out_specs=pl.BlockSpec((1, rows, tn), lambda l, j: (l, 0, j)),
        out_shape=jax.ShapeDtypeStruct((depth, rows, n), F32),
        compiler_params=_params("arbitrary", "arbitrary"),
        name="adaln_modulation",
    )(c_pad, ada_w, ada_b.reshape(depth, 1, n))
    return out[:, :b].reshape(depth, b, N_MOD, d)


def _inproj_kernel(x_ref, mod_ref, g_ref, w_ref, b_ref, o_ref, h_ref, *, rows):
    tm = x_ref.shape[0]
    j = pl.program_id(1)

    @pl.when(j == 0)
    def _():
        _norm_mod_into(x_ref, h_ref, g_ref[...] * (1.0 + mod_ref[0, 1:2, :]), mod_ref[0, 0:1, :],
                       rows)

    def project(act):
        z = jnp.dot(h_ref[...], w_ref[...].astype(BF16), preferred_element_type=F32) + b_ref[...]
        o_ref[...] = act(z).astype(o_ref.dtype)

    blk = j // _BLK
    is_gelu = blk == OUT_GATE
    is_sigmoid = jnp.logical_or(blk == OUT_GA, blk == OUT_GB)

    @pl.when(is_gelu)
    def _():
        project(_gelu_tanh)

    @pl.when(is_sigmoid)
    def _():
        project(_sigmoid_tanh)

    @pl.when(jnp.logical_not(jnp.logical_or(is_gelu, is_sigmoid)))
    def _():
        project(lambda z: z)


def _in_proj(x, mod, g, w, b, layer, seq, tm):
    t, d = x.shape
    tn = PROJ_TN
    per_seq = seq // tm
    if w.ndim == 3:
        w_spec = pl.BlockSpec((None, d, tn), lambda i, j: (layer, 0, _src_col_block(j)))
    else:
        w_spec = pl.BlockSpec((d, tn), lambda i, j: (0, _src_col_block(j)))
    return pl.pallas_call(
        functools.partial(_inproj_kernel, rows=min(tm, 128)),
        grid=(t // tm, N_PROJ_BLK),
        in_specs=[
            pl.BlockSpec((tm, d), lambda i, j: (i, 0)),
            pl.BlockSpec((1, N_MOD, d), lambda i, j: (i // per_seq, 0, 0)),
            pl.BlockSpec((1, d), lambda i, j: (0, 0)),
            w_spec,
            pl.BlockSpec((None, 1, tn), lambda i, j: (layer, 0, _src_col_block(j))),
        ],
        out_specs=pl.BlockSpec((tm, tn), lambda i, j: (i, j)),
        out_shape=jax.ShapeDtypeStruct((t, C_IN), BF16),
        scratch_shapes=[pltpu.VMEM((tm, d), BF16)],
        compiler_params=_params("arbitrary", "arbitrary"),
        name="norm_in_proj",
    )(x, mod, g.reshape(1, d), w, b)


def _lru_kernel(ug_ref, mg_ref, convw_ref, convb_ref, wg_ref, lam_ref,
                wout_ref, wa32_ref, wo32_ref, o_ref, wa16_ref, wo16_ref,
                ubuf, a_s, b_s, y_s, hcarry, wout_bf, *, group, per_seq):
    wa16_ref[...] = wa32_ref[...].astype(BF16)
    wo16_ref[...] = wo32_ref[...].astype(BF16)
    ts = ug_ref.shape[0]
    sub = ts // SUBLANES
    pitch = sub + 4
    halo = SUBLANES
    g_idx = pl.program_id(0)
    first = g_idx % per_seq == 0

    @pl.when(g_idx == 0)
    def _():
        ubuf[:, ts:ts + halo, :] = jnp.zeros((LRU_HEADS, halo, LANES), F32)
        hcarry[...] = jnp.zeros_like(hcarry)
        y_s[...] = jnp.zeros_like(y_s)
        wout_bf[...] = wout_ref[...].astype(BF16)

    ubuf[:, 0:halo, :] = jnp.where(first, 0.0, ubuf[:, ts:ts + halo, :])
    hcarry[...] = jnp.where(first, 0.0, hcarry[...])

    c_softplus = LRU_C * _softplus(-lam_ref[...])
    bias_cols = (lax.broadcasted_iota(jnp.int32, (ts, LRU_HEAD_DIM), 1)
                 < N_BIAS_ROWS).astype(BF16)
    out_cols = MXU_WIDTH
    heads_per_chunk = out_cols // LRU_HEAD_DIM

    for hd in range(LRU_HEADS):
        if hd % heads_per_chunk == 0:
            c0 = (hd // heads_per_chunk) * out_cols
            out = jnp.dot(y_s[...], wout_bf[:, c0:c0 + out_cols], preferred_element_type=F32)
            o_ref[:, c0:c0 + out_cols] = (
                mg_ref[:, c0:c0 + out_cols].astype(F32) * out).astype(o_ref.dtype)
        cs = slice(hd * LRU_HEAD_DIM, (hd + 1) * LRU_HEAD_DIM)
        ubuf[hd, halo:halo + ts, :] = ug_ref[:, cs].astype(F32)
        uc = convb_ref[:, cs]
        for k in range(CONV_W):
            off = halo - (CONV_W - 1) + k
            uc = uc + convw_ref[k:k + 1, cs] * ubuf[hd, off:off + ts, :]
        g = jnp.dot(jnp.concatenate([uc.astype(BF16), bias_cols], axis=1), wg_ref[hd],
                    preferred_element_type=F32)
        r = _sigmoid_tanh(g[:, :LRU_HEAD_DIM])
        i = _sigmoid_tanh(g[:, LRU_HEAD_DIM:])
        a = jnp.exp2(r * (c_softplus[:, cs] * -LOG2_E))
        one_m_a2 = (1.0 + a * a) * jnp.tanh(r * c_softplus[:, cs])
        beta = jnp.where(one_m_a2 > 0.0, one_m_a2 * lax.rsqrt(one_m_a2), 0.0)
        b = beta * (i * uc)
        for s in range(SUBLANES):
            a_s[hd, s * pitch:s * pitch + sub, :] = a[s * sub:(s + 1) * sub, :]
            b_s[hd, s * pitch:s * pitch + sub, :] = b[s * sub:(s + 1) * sub, :]

    row = lax.broadcasted_iota(jnp.int32, (SUBLANES, LANES), 0)
    zeros = jnp.zeros((SUBLANES, LANES), F32)
    ones = jnp.ones((SUBLANES, LANES), F32)

    for h0 in range(0, LRU_HEADS, group):
        heads = range(h0, h0 + group)

        def strided(ref, hd, m):
            return ref[hd, pl.ds(m, SUBLANES, stride=pitch), :]

        def pass1(m, carry):
            out = []
            for n, hd in enumerate(heads):
                a_m = strided(a_s, hd, m)
                out += [a_m * carry[2 * n] + strided(b_s, hd, m), carry[2 * n + 1] * a_m]
            return tuple(out)

        ends = lax.fori_loop(0, sub, pass1, (zeros, ones) * group, unroll=SCAN_UNROLL)

        starts = []
        for n, hd in enumerate(heads):
            e, q = ends[2 * n], ends[2 * n + 1]
            for d in (1, 2, 4):
                keep = row >= d
                e_prev = pltpu.roll(e, d, 0)
                q_prev = pltpu.roll(q, d, 0)
                e = jnp.where(keep, q * e_prev, 0.0) + e
                q = jnp.where(keep, q * q_prev, q)
            cs = slice(hd * LRU_HEAD_DIM, (hd + 1) * LRU_HEAD_DIM)
            h_in = hcarry[:, cs]
            end_state = e + q * h_in
            hcarry[:, cs] = end_state[SUBLANES - 1:SUBLANES, :]
            starts.append(jnp.where(row == 0, h_in, pltpu.roll(end_state, 1, 0)))

        def pass2(m, carry):
            out = []
            for n, hd in enumerate(heads):
                h = strided(a_s, hd, m) * carry[n] + strided(b_s, hd, m)
                b_s[hd, pl.ds(m, SUBLANES, stride=pitch), :] = h
                out.append(h)
            return tuple(out)

        lax.fori_loop(0, sub, pass2, tuple(starts), unroll=SCAN_UNROLL)

    for hd in range(LRU_HEADS):
        cs = slice(hd * LRU_HEAD_DIM, (hd + 1) * LRU_HEAD_DIM)
        h = jnp.concatenate(
            [b_s[hd, s * pitch:s * pitch + sub, :] for s in range(SUBLANES)], axis=0)
        gs = slice(D_MODEL + hd * LRU_HEAD_DIM, D_MODEL + (hd + 1) * LRU_HEAD_DIM)
        y_s[:, cs] = (h * ug_ref[:, gs].astype(F32)).astype(BF16)


def _lru_branch(proj, conv_w, conv_b, wg, lam, w_out, w_attn, w_o, layer, batch, seq, ts):
    t = proj.shape[0]
    d = D_MODEL
    per_seq = seq // ts
    n_tiles = batch * per_seq
    pitch = ts // SUBLANES + 4
    cur = lambda g: jnp.minimum(g, n_tiles - 1)
    prev = lambda g: jnp.maximum(g - 1, 0)
    wa_in, wa_out, wa_shape = _cast_specs(w_attn.shape, layer, lambda g: g, n_tiles + 1)
    wo_in, wo_out, wo_shape = _cast_specs(w_o.shape, layer, lambda g: g, n_tiles + 1)
    return pl.pallas_call(
        functools.partial(_lru_kernel, group=8, per_seq=per_seq),
        grid=(n_tiles + 1,),
        in_specs=[
            pl.BlockSpec((ts, 2 * d), lambda g: (cur(g), OUT_U // 2)),
            pl.BlockSpec((ts, d), lambda g: (prev(g), OUT_GA)),
            _resident((CONV_W, d), lambda g: (0, 0)),
            _resident((1, d), lambda g: (0, 0)),
            _resident((LRU_HEADS, 2 * LRU_HEAD_DIM, 2 * LRU_HEAD_DIM), lambda g: (0, 0, 0)),
            _resident((1, d), lambda g: (0, 0)),
            _resident((None, d, d), lambda g: (layer, 0, 0)),
            wa_in,
            wo_in,
        ],
        out_specs=[pl.BlockSpec((ts, d), lambda g: (prev(g), 0)), wa_out, wo_out],
        out_shape=[jax.ShapeDtypeStruct((t, d), BF16), wa_shape, wo_shape],
        scratch_shapes=[
            pltpu.VMEM((LRU_HEADS, ts + SUBLANES, LANES), F32),
            pltpu.VMEM((LRU_HEADS, SUBLANES * pitch, LANES), F32),
            pltpu.VMEM((LRU_HEADS, SUBLANES * pitch, LANES), F32),
            pltpu.VMEM((ts, d), BF16),
            pltpu.VMEM((1, d), F32),
            pltpu.VMEM((d, d), BF16),
        ],
        compiler_params=_params("arbitrary"),
        name="rglru_branch",
    )(proj, proj, conv_w, conv_b.reshape(1, d), wg, lam.reshape(1, d), w_out,
      w_attn, w_o)


def _attn_kernel(sinks_ref, q_ref, kv_ref, kvp_ref, ma_ref, gb_ref, x_ref,
                 mod_ref, wa_ref, wo_ref, wfi32_ref, wfo32_ref, o_ref, wfi16_ref, wfo16_ref, y_s):
    wfi16_ref[...] = wfi32_ref[...].astype(BF16)
    wfo16_ref[...] = wfo32_ref[...].astype(BF16)

    tq = q_ref.shape[0]
    n_blk = tq // WINDOW
    first = pl.program_id(1) == 0
    rows = GQA_GROUP * WINDOW

    qi = lax.broadcasted_iota(jnp.int32, (rows, WINDOW), 0) % WINDOW
    kk = lax.broadcasted_iota(jnp.int32, (rows, WINDOW), 1)
    in_cur = kk <= qi
    scale = HEAD_DIM ** -0.5
    nt_dims = (((1,), (1,)), ((), ()))
    ones_blk = jnp.ones((2 * WINDOW, HEAD_DIM), BF16)

    for h in range(N_KV_HEADS):
        hs = slice(h * HEAD_DIM, (h + 1) * HEAD_DIM)
        vs = slice(KV_WIDTH + h * HEAD_DIM, KV_WIDTH + (h + 1) * HEAD_DIM)
        sink = jnp.concatenate(
            [jnp.full((WINDOW, 1), sinks_ref[h * GQA_GROUP + g], F32) for g in range(GQA_GROUP)],
            axis=0)
        for n in range(n_blk):
            rs = slice(n * WINDOW, (n + 1) * WINDOW)
            q4 = jnp.concatenate(
                [q_ref[rs, (h * GQA_GROUP + g) * HEAD_DIM:(h * GQA_GROUP + g + 1) * HEAD_DIM]
                 for g in range(GQA_GROUP)], axis=0)
            if n == 0:
                k_prev, v_prev = kvp_ref[:, hs], kvp_ref[:, vs]
            else:
                ps = slice((n - 1) * WINDOW, n * WINDOW)
                k_prev, v_prev = kv_ref[ps, hs], kv_ref[ps, vs]
            k_cat = jnp.concatenate([k_prev, kv_ref[rs, hs]], axis=0)
            v_cat = jnp.concatenate([v_prev, kv_ref[rs, vs]], axis=0)
            s = lax.dot_general(q4, k_cat, nt_dims, preferred_element_type=F32)
            s_prev = s[:, :WINDOW]
            if n == 0:
                s_prev = jnp.where(first, -jnp.inf, s_prev)
            sc = jnp.where(in_cur, s[:, WINDOW:], s_prev) * scale
            m = jnp.maximum(jnp.max(sc, axis=-1, keepdims=True), sink)
            p = jnp.exp(sc - m)
            p_cat = jnp.concatenate(
                [jnp.where(in_cur, 0.0, p), jnp.where(in_cur, p, 0.0)], axis=1).astype(BF16)
            o_ext = jnp.dot(p_cat, jnp.concatenate([v_cat, ones_blk], axis=1),
                            preferred_element_type=F32)
            denom = o_ext[:, HEAD_DIM:] + jnp.exp(sink - m)
            o = o_ext[:, :HEAD_DIM] / denom
            for g in range(GQA_GROUP):
                cs = slice((h * GQA_GROUP + g) * HEAD_DIM, (h * GQA_GROUP + g + 1) * HEAD_DIM)
                y_s[rs, cs] = o[g * WINDOW:(g + 1) * WINDOW].astype(BF16)

    attn = jnp.dot(y_s[...], wa_ref[...], preferred_element_type=F32)
    merged = gb_ref[...].astype(F32) * attn + ma_ref[...].astype(F32)
    r = jnp.dot(merged.astype(BF16), wo_ref[...], preferred_element_type=F32)
    o_ref[...] = x_ref[...] + mod_ref[0, 2:3, :] * r


def _attn_merge(proj, merged_a, x, sinks, mod, w_attn, w_o, w_ffn_in, w_ffn_out, layer,
                batch, seq, tq):
    t, d = x.shape
    per_seq = seq // tq
    n_blk = tq // WINDOW
    row = lambda b, s: b * per_seq + s
    prev_blk = lambda b, s: jnp.maximum(row(b, s) * n_blk - 1, 0)
    fi_in, fi_out, fi_shape = _cast_specs(w_ffn_in.shape, layer, row, batch * per_seq)
    fo_in, fo_out, fo_shape = _cast_specs(w_ffn_out.shape, layer, row, batch * per_seq)
    return pl.pallas_call(
        _attn_kernel,
        grid=(batch, per_seq),
        in_specs=[
            pl.BlockSpec(memory_space=pltpu.SMEM),
            pl.BlockSpec((tq, d), lambda b, s: (row(b, s), OUT_Q)),
            pl.BlockSpec((tq, 2 * KV_WIDTH), lambda b, s: (row(b, s), OUT_KV)),
            pl.BlockSpec((WINDOW, 2 * KV_WIDTH), lambda b, s: (prev_blk(b, s), OUT_KV)),
            pl.BlockSpec((tq, d), lambda b, s: (row(b, s), 0)),
            pl.BlockSpec((tq, d), lambda b, s: (row(b, s), OUT_GB)),
            pl.BlockSpec((tq, d), lambda b, s: (row(b, s), 0)),
            pl.BlockSpec((1, N_MOD, d), lambda b, s: (b, 0, 0)),
            _resident((d, d), lambda b, s: (0, 0)),
            _resident((d, d), lambda b, s: (0, 0)),
            fi_in,
            fo_in,
        ],
        out_specs=[pl.BlockSpec((tq, d), lambda b, s: (row(b, s), 0)), fi_out, fo_out],
        out_shape=[jax.ShapeDtypeStruct((t, d), F32), fi_shape, fo_shape],
        scratch_shapes=[pltpu.VMEM((tq, d), BF16)],
        compiler_params=_params("arbitrary", "arbitrary"),
        name="swa_merge_out_proj",
    )(sinks, proj, proj, proj, merged_a, proj, x, mod, w_attn, w_o,
      w_ffn_in, w_ffn_out)


def _ffn_kernel(x_ref, mod_ref, g_ref, wg_ref, wu_ref, wo_ref, fg_ref, *rest, rows, final_norm,
                side_cast):
    if side_cast:
        w32_ref, o_ref, w16_ref, h_ref = rest
        w16_ref[...] = w32_ref[...].astype(BF16)
    else:
        o_ref, h_ref = rest
    acc_ref = o_ref
    tm, d = x_ref.shape
    th = wg_ref.shape[1]
    j = pl.program_id(1)

    @pl.when(j == 0)
    def _():
        _norm_mod_into(x_ref, h_ref, g_ref[...] * (1.0 + mod_ref[0, 4:5, :]), mod_ref[0, 3:4, :],
                       rows, also_zero=acc_ref)

    h = h_ref[...]
    acts = []
    for c0 in range(0, th, MXU_WIDTH):
        gate = jnp.dot(h, wg_ref[:, c0:c0 + MXU_WIDTH], preferred_element_type=F32)
        up = jnp.dot(h, wu_ref[:, c0:c0 + MXU_WIDTH], preferred_element_type=F32)
        acts.append((gate * _sigmoid_tanh(gate) * up).astype(BF16))
    act = jnp.concatenate(acts, axis=1)
    oc = 2 * MXU_WIDTH
    for c0 in range(0, d, oc):
        acc_ref[:, c0:c0 + oc] += jnp.dot(act, wo_ref[:, c0:c0 + oc],
                                           preferred_element_type=F32)

    @pl.when(j == pl.num_programs(1) - 1)
    def _():
        def body(i, carry):
            r = pl.ds(pl.multiple_of(i * rows, rows), rows)
            y = x_ref[r, :] + mod_ref[0, 5:6, :] * acc_ref[r, :]
            if final_norm:
                inv = lax.rsqrt(jnp.mean(y * y, axis=-1, keepdims=True) + EPS)
                y = (y * inv) * fg_ref[...]
            o_ref[r, :] = y
            return carry
        lax.fori_loop(0, tm // rows, body, 0)


def _ffn(x, mod, g, w_in, w_out, final_g, final_norm, seq, tm, th, w_next=None, next_layer=None):
    t, d = x.shape
    hidden = w_out.shape[0]
    per_seq = seq // tm
    n_h = hidden // th
    n_tiles = t // tm
    in_specs = [
        pl.BlockSpec((tm, d), lambda i, j: (i, 0)),
        pl.BlockSpec((1, N_MOD, d), lambda i, j: (i // per_seq, 0, 0)),
        pl.BlockSpec((1, d), lambda i, j: (0, 0)),
        pl.BlockSpec((d, th), lambda i, j: (0, j)),
        pl.BlockSpec((d, th), lambda i, j: (0, j + n_h)),
        pl.BlockSpec((th, d), lambda i, j: (j, 0)),
        pl.BlockSpec((1, d), lambda i, j: (0, 0)),
    ]
    out_specs = [pl.BlockSpec((tm, d), lambda i, j: (i, 0))]
    out_shape = [jax.ShapeDtypeStruct((t, d), F32)]
    args = [x, mod, g.reshape(1, d), w_in, w_in, w_out, final_g.reshape(1, d)]
    side_cast = w_next is not None
    if side_cast:
        c_in, c_out, c_shape = _cast_specs(w_next.shape, next_layer, lambda i, j: i * n_h + j,
                                           n_tiles * n_h)
        in_specs.append(c_in)
        out_specs.append(c_out)
        out_shape.append(c_shape)
        args.append(w_next)
    res = pl.pallas_call(
        functools.partial(_ffn_kernel, rows=min(tm, 128), final_norm=final_norm,
                          side_cast=side_cast),
        grid=(n_tiles, n_h),
        in_specs=in_specs,
        out_specs=out_specs,
        out_shape=out_shape,
        scratch_shapes=[pltpu.VMEM((tm, d), BF16)],
        compiler_params=_params("arbitrary", "arbitrary"),
        name="norm_swiglu",
    )(*args)
    return (res[0], res[1]) if side_cast else (res[0], None)


def kernel(x, c, ada_w, ada_b, norm1_g, w_in, b_in, conv_w, conv_b, lru_wa, lru_ba, lru_wx,
           lru_bx, lru_lambda, sinks, w_lru_out, w_attn_out, w_o, norm2_g, w_ffn_in, w_ffn_out,
           final_g):
    batch, seq, d = x.shape
    assert d == D_MODEL and seq % WINDOW == 0
    depth = ada_w.shape[0]
    t = batch * seq
    tm_proj = min(seq, TM_PROJ)
    tm_ffn = min(seq, TM_FFN)
    ts_lru = min(seq, TS_LRU)
    tq_attn = min(seq, TQ_ATTN)

    bg = jnp.concatenate([lru_ba.reshape(depth, LRU_HEADS, 1, LRU_HEAD_DIM),
                          lru_bx.reshape(depth, LRU_HEADS, 1, LRU_HEAD_DIM)], axis=-1)
    bg_hi = bg.astype(BF16)
    bg_lo = (bg - bg_hi.astype(F32)).astype(BF16)
    wg = jnp.concatenate(
        [jnp.concatenate([lru_wa, lru_wx], axis=-1).astype(BF16), bg_hi, bg_lo,
         jnp.zeros((depth, LRU_HEADS, LRU_HEAD_DIM - N_BIAS_ROWS, 2 * LRU_HEAD_DIM), BF16)],
        axis=2)
    b_in3 = b_in.reshape(depth, 1, C_IN)

    mod = _modulation(c, ada_w, ada_b)
    xf = x.reshape(t, d)
    w_in_l = w_in
    for l in range(depth):
        last = l == depth - 1
        proj = _in_proj(xf, mod[l], norm1_g[l], w_in_l, b_in3, l, seq, tm_proj)
        merged_a, w_attn_bf, w_o_bf = _lru_branch(
            proj, conv_w[l], conv_b[l], wg[l], lru_lambda[l], w_lru_out, w_attn_out, w_o,
            l, batch, seq, ts_lru)
        xf, w_ffn_in_bf, w_ffn_out_bf = _attn_merge(
            proj, merged_a, xf, sinks[l], mod[l], w_attn_bf, w_o_bf, w_ffn_in, w_ffn_out, l,
            batch, seq, tq_attn)
        xf, w_in_l = _ffn(xf, mod[l], norm2_g[l], w_ffn_in_bf, w_ffn_out_bf, final_g, last, seq,
                          tm_ffn, FFN_TH, w_next=None if last else w_in,
                          next_layer=None if last else l + 1)
    return xf.reshape(batch, seq, d)
```

```python
import functools

import jax
import jax.numpy as jnp
from jax import lax
from jax.experimental import pallas as pl
from jax.experimental.pallas import tpu as pltpu

F32 = jnp.float32
BF16 = jnp.bfloat16

D_MODEL = 2048
LRU_HEADS = 16
LRU_HEAD_DIM = D_MODEL // LRU_HEADS
CONV_W = 4
LRU_C = 8.0
LOG2_E = 1.4426950408889634
HEAD_DIM = 128
N_Q_HEADS = 16
N_KV_HEADS = 4
GQA_GROUP = N_Q_HEADS // N_KV_HEADS
WINDOW = 128
KV_WIDTH = N_KV_HEADS * HEAD_DIM
N_MOD = 6
EPS = 1e-6
C_IN = 5 * D_MODEL + 2 * KV_WIDTH

PROJ_TN = 2 * KV_WIDTH
_BLK = D_MODEL // PROJ_TN
_KV_BLK = 2 * KV_WIDTH // PROJ_TN
OUT_U, OUT_GATE, OUT_Q, OUT_GA, OUT_GB = 0, 1, 2, 3, 4
OUT_K = 5 * D_MODEL // KV_WIDTH
OUT_V = OUT_K + 1
N_PROJ_BLK = C_IN // PROJ_TN

SUBLANES = 8
LANES = 128
MXU_WIDTH = 256
VMEM_LIMIT = 56 * 1024 * 1024
N_BIAS_ROWS = 2

TM_PROJ = 1024
TM_FFN = 1024
TS_LRU = 256
TQ_ATTN = 256
FFN_TH = 2 * MXU_WIDTH
ADALN_TN = 1024
SCAN_UNROLL = 8


def _src_col_block(j):
    return jnp.where(j < 3 * _BLK, j, jnp.where(j < 5 * _BLK, j + _KV_BLK, j - 2 * _BLK))


def _params(*sem):
    return pltpu.CompilerParams(dimension_semantics=sem, vmem_limit_bytes=VMEM_LIMIT)


def _cast_plan(rows, n_steps):
    k = n_steps
    while rows % k or (rows // k) % (2 * SUBLANES):
        k -= 1
    return rows // k, k


def _cast_specs(shape, layer, step_of, n_steps):
    rows, cols = shape[1:]
    blk, n_blk = _cast_plan(rows, n_steps)
    walk = lambda *g: jnp.minimum(step_of(*g), n_blk - 1)
    return (pl.BlockSpec((None, blk, cols), lambda *g: (layer, walk(*g), 0)),
            pl.BlockSpec((blk, cols), lambda *g: (walk(*g), 0)),
            jax.ShapeDtypeStruct((rows, cols), BF16))


def _resident(shape, index_map):
    return pl.BlockSpec(shape, index_map, pipeline_mode=pl.Buffered(1))


def _norm_mod_into(x_ref, h_ref, gain, shift, rows, also_zero=None):
    tm, d = x_ref.shape
    group = 2 * SUBLANES

    def body(i, carry):
        for q in range(rows // group):
            r = pl.ds(pl.multiple_of(i * rows + q * group, group), group)
            x = x_ref[r, :]
            inv = lax.rsqrt(jnp.mean(x * x, axis=-1, keepdims=True) + EPS)
            h_ref[r, :] = ((x * inv) * gain + shift).astype(BF16)
            if also_zero is not None:
                also_zero[r, :] = jnp.zeros((group, d), F32)
        return carry

    lax.fori_loop(0, tm // rows, body, 0)


_SQRT_2_OVER_PI = 0.7978845608028654


def _gelu_tanh(x):
    hx = 0.5 * x
    return hx + hx * jnp.tanh(x * (_SQRT_2_OVER_PI + (_SQRT_2_OVER_PI * 0.044715) * (x * x)))


def _sigmoid_tanh(x):
    return 0.5 * jnp.tanh(0.5 * x) + 0.5


def _softplus(z):
    return jnp.maximum(z, 0.0) + jnp.log1p(jnp.exp(-jnp.abs(z)))


def _mod_kernel(c_ref, w_ref, b_ref, o_ref):
    c = c_ref[...]
    act = (c * jax.nn.sigmoid(c)).astype(BF16)
    o_ref[0] = jnp.dot(act, w_ref[0].astype(BF16), preferred_element_type=F32) + b_ref[0]


def _modulation(c, ada_w, ada_b):
    depth, d, n = ada_w.shape
    tn = ADALN_TN
    b = c.shape[0]
    rows = -(-b // SUBLANES) * SUBLANES
    c_pad = jnp.zeros((rows, d), F32).at[:b].set(c)
    out = pl.pallas_call(
        _mod_kernel,
        grid=(depth, n // tn),
        in_specs=[
            pl.BlockSpec((rows, d), lambda l, j: (0, 0)),
            pl.BlockSpec((1, d, tn), lambda l, j: (l, 0, j)),
            pl.BlockSpec((1, 1, tn), lambda l, j: (l, 0, j)),
        ],
        out_specs=pl.BlockSpec((1, rows, tn), lambda l, j: (l, 0, j)),
        out_shape=jax.ShapeDtypeStruct((depth, rows, n), F32),
        compiler_params=_params("arbitrary", "arbitrary"),
        name="adaln_modulation",
    )(c_pad, ada_w, ada_b.reshape(depth, 1, n))
    return out[:, :b].reshape(depth, b, N_MOD, d)


def _inproj_kernel(x_ref, mod_ref, g_ref, w_ref, b_ref, o_ref, h_ref, *, rows):
    tm = x_ref.shape[0]
    j = pl.program_id(1)

    @pl.when(j == 0)
    def _():
        _norm_mod_into(x_ref, h_ref, g_ref[...] * (1.0 + mod_ref[0, 1:2, :]), mod_ref[0, 0:1, :],
                       rows)

    def project(act):
        z = jnp.dot(h_ref[...], w_ref[...].astype(BF16), preferred_element_type=F32) + b_ref[...]
        o_ref[...] = act(z).astype(o_ref.dtype)

    blk = j // _BLK
    is_gelu = blk == OUT_GATE
    is_sigmoid = jnp.logical_or(blk == OUT_GA, blk == OUT_GB)

    @pl.when(is_gelu)
    def _():
        project(_gelu_tanh)

    @pl.when(is_sigmoid)
    def _():
        project(_sigmoid_tanh)

    @pl.when(jnp.logical_not(jnp.logical_or(is_gelu, is_sigmoid)))
    def _():
        project(lambda z: z)


def _in_proj(x, mod, g, w, b, layer, seq, tm):
    t, d = x.shape
    tn = PROJ_TN
    per_seq = seq // tm
    if w.ndim == 3:
        w_spec = pl.BlockSpec((None, d, tn), lambda i, j: (layer, 0, _src_col_block(j)))
    else:
        w_spec = pl.BlockSpec((d, tn), lambda i, j: (0, _src_col_block(j)))
    return pl.pallas_call(
        functools.partial(_inproj_kernel, rows=min(tm, 128)),
        grid=(t // tm, N_PROJ_BLK),
        in_specs=[
            pl.BlockSpec((tm, d), lambda i, j: (i, 0)),
            pl.BlockSpec((1, N_MOD, d), lambda i, j: (i // per_seq, 0, 0)),
            pl.BlockSpec((1, d), lambda i, j: (0, 0)),
            w_spec,
            pl.BlockSpec((None, 1, tn), lambda i, j: (layer, 0, _src_col_block(j))),
        ],
        out_specs=pl.BlockSpec((tm, tn), lambda i, j: (i, j)),
        out_shape=jax.ShapeDtypeStruct((t, C_IN), BF16),
        scratch_shapes=[pltpu.VMEM((tm, d), BF16)],
        compiler_params=_params("arbitrary", "arbitrary"),
        name="norm_in_proj",
    )(x, mod, g.reshape(1, d), w, b)


def _lru_kernel(u_ref, gate_ref, mg_ref, convw_ref, convb_ref, wg_ref, lam_ref, wout_ref,
                *rest, group, per_seq, n_cast):
    w32_refs, o_ref = rest[:n_cast], rest[n_cast]
    w16_refs = rest[n_cast + 1:2 * n_cast + 1]
    ubuf, a_s, b_s, y_s, hcarry, wout_bf = rest[2 * n_cast + 1:]
    for w32_ref, w16_ref in zip(w32_refs, w16_refs):
        w16_ref[...] = w32_ref[...].astype(BF16)
    ts = u_ref.shape[0]
    sub = ts // SUBLANES
    pitch = sub + 4
    halo = SUBLANES
    g_idx = pl.program_id(0)
    first = g_idx % per_seq == 0

    @pl.when(g_idx == 0)
    def _():
        ubuf[:, ts:ts + halo, :] = jnp.zeros((LRU_HEADS, halo, LANES), F32)
        hcarry[...] = jnp.zeros_like(hcarry)
        y_s[...] = jnp.zeros_like(y_s)
        wout_bf[...] = wout_ref[...].astype(BF16)

    ubuf[:, 0:halo, :] = jnp.where(first, 0.0, ubuf[:, ts:ts + halo, :])
    hcarry[...] = jnp.where(first, 0.0, hcarry[...])

    c_softplus = LRU_C * _softplus(-lam_ref[...])
    bias_cols = (lax.broadcasted_iota(jnp.int32, (ts, LRU_HEAD_DIM), 1)
                 < N_BIAS_ROWS).astype(BF16)
    out_cols = MXU_WIDTH
    heads_per_chunk = out_cols // LRU_HEAD_DIM

    for hd in range(LRU_HEADS):
        if hd % heads_per_chunk == 0:
            c0 = (hd // heads_per_chunk) * out_cols
            out = jnp.dot(y_s[...], wout_bf[:, c0:c0 + out_cols], preferred_element_type=F32)
            o_ref[:, c0:c0 + out_cols] = (
                mg_ref[:, c0:c0 + out_cols].astype(F32) * out).astype(o_ref.dtype)
        cs = slice(hd * LRU_HEAD_DIM, (hd + 1) * LRU_HEAD_DIM)
        ubuf[hd, halo:halo + ts, :] = u_ref[:, cs].astype(F32)
        uc = convb_ref[:, cs]
        for k in range(CONV_W):
            off = halo - (CONV_W - 1) + k
            uc = uc + convw_ref[k:k + 1, cs] * ubuf[hd, off:off + ts, :]
        g = jnp.dot(jnp.concatenate([uc.astype(BF16), bias_cols], axis=1), wg_ref[hd],
                    preferred_element_type=F32)
        r = _sigmoid_tanh(g[:, :LRU_HEAD_DIM])
        i = _sigmoid_tanh(g[:, LRU_HEAD_DIM:])
        a = jnp.exp2(r * (c_softplus[:, cs] * -LOG2_E))
        one_m_a2 = (1.0 + a * a) * jnp.tanh(r * c_softplus[:, cs])
        beta = jnp.where(one_m_a2 > 0.0, one_m_a2 * lax.rsqrt(one_m_a2), 0.0)
        b = beta * (i * uc)
        for s in range(SUBLANES):
            a_s[hd, s * pitch:s * pitch + sub, :] = a[s * sub:(s + 1) * sub, :]
            b_s[hd, s * pitch:s * pitch + sub, :] = b[s * sub:(s + 1) * sub, :]

    row = lax.broadcasted_iota(jnp.int32, (SUBLANES, LANES), 0)
    zeros = jnp.zeros((SUBLANES, LANES), F32)
    ones = jnp.ones((SUBLANES, LANES), F32)

    for h0 in range(0, LRU_HEADS, group):
        heads = range(h0, h0 + group)

        def strided(ref, hd, m):
            return ref[hd, pl.ds(m, SUBLANES, stride=pitch), :]

        def pass1(m, carry):
            out = []
            for n, hd in enumerate(heads):
                a_m = strided(a_s, hd, m)
                out += [a_m * carry[2 * n] + strided(b_s, hd, m), carry[2 * n + 1] * a_m]
            return tuple(out)

        ends = lax.fori_loop(0, sub, pass1, (zeros, ones) * group, unroll=SCAN_UNROLL)

        starts = []
        for n, hd in enumerate(heads):
            e, q = ends[2 * n], ends[2 * n + 1]
            for d in (1, 2, 4):
                keep = row >= d
                e_prev = pltpu.roll(e, d, 0)
                q_prev = pltpu.roll(q, d, 0)
                e = jnp.where(keep, q * e_prev, 0.0) + e
                q = jnp.where(keep, q * q_prev, q)
            cs = slice(hd * LRU_HEAD_DIM, (hd + 1) * LRU_HEAD_DIM)
            h_in = hcarry[:, cs]
            end_state = e + q * h_in
            hcarry[:, cs] = end_state[SUBLANES - 1:SUBLANES, :]
            starts.append(jnp.where(row == 0, h_in, pltpu.roll(end_state, 1, 0)))

        def pass2(m, carry):
            out = []
            for n, hd in enumerate(heads):
                h = strided(a_s, hd, m) * carry[n] + strided(b_s, hd, m)
                b_s[hd, pl.ds(m, SUBLANES, stride=pitch), :] = h
                out.append(h)
            return tuple(out)

        lax.fori_loop(0, sub, pass2, tuple(starts), unroll=SCAN_UNROLL)

    for hd in range(LRU_HEADS):
        cs = slice(hd * LRU_HEAD_DIM, (hd + 1) * LRU_HEAD_DIM)
        h = jnp.concatenate(
            [b_s[hd, s * pitch:s * pitch + sub, :] for s in range(SUBLANES)], axis=0)
        y_s[:, cs] = (h * gate_ref[:, cs].astype(F32)).astype(BF16)


def _lru_branch(proj, conv_w, conv_b, wg, lam, w_out, casts, layer, batch, seq, ts):
    t = proj.shape[0]
    d = D_MODEL
    per_seq = seq // ts
    n_tiles = batch * per_seq
    pitch = ts // SUBLANES + 4
    cur = lambda g: jnp.minimum(g, n_tiles - 1)
    prev = lambda g: jnp.maximum(g - 1, 0)
    cast_specs = [_cast_specs(w.shape, wl, lambda g: g, n_tiles + 1) for w, wl in casts]
    return pl.pallas_call(
        functools.partial(_lru_kernel, group=8, per_seq=per_seq, n_cast=len(casts)),
        grid=(n_tiles + 1,),
        in_specs=[
            pl.BlockSpec((ts, d), lambda g: (cur(g), OUT_U)),
            pl.BlockSpec((ts, d), lambda g: (cur(g), OUT_GATE)),
            pl.BlockSpec((ts, d), lambda g: (prev(g), OUT_GA)),
            _resident((CONV_W, d), lambda g: (0, 0)),
            _resident((1, d), lambda g: (0, 0)),
            _resident((LRU_HEADS, 2 * LRU_HEAD_DIM, 2 * LRU_HEAD_DIM), lambda g: (0, 0, 0)),
            _resident((1, d), lambda g: (0, 0)),
            _resident((None, d, d), lambda g: (layer, 0, 0)),
            *[c[0] for c in cast_specs],
        ],
        out_specs=[pl.BlockSpec((ts, d), lambda g: (prev(g), 0)), *[c[1] for c in cast_specs]],
        out_shape=[jax.ShapeDtypeStruct((t, d), BF16), *[c[2] for c in cast_specs]],
        scratch_shapes=[
            pltpu.VMEM((LRU_HEADS, ts + SUBLANES, LANES), F32),
            pltpu.VMEM((LRU_HEADS, SUBLANES * pitch, LANES), F32),
            pltpu.VMEM((LRU_HEADS, SUBLANES * pitch, LANES), F32),
            pltpu.VMEM((ts, d), BF16),
            pltpu.VMEM((1, d), F32),
            pltpu.VMEM((d, d), BF16),
        ],
        compiler_params=_params("arbitrary"),
        name="rglru_branch",
    )(proj, proj, proj, conv_w, conv_b.reshape(1, d), wg, lam.reshape(1, d), w_out,
      *[w for w, _ in casts])


def _attn_kernel(sinks_ref, q_ref, k_ref, v_ref, kp_ref, vp_ref, ma_ref, gb_ref, x_ref,
                 mod_ref, wa_ref, wo_ref, wfi32_ref, wfo32_ref, o_ref, wfi16_ref, wfo16_ref, y_s):
    wfi16_ref[...] = wfi32_ref[...].astype(BF16)
    wfo16_ref[...] = wfo32_ref[...].astype(BF16)

    tq = q_ref.shape[0]
    n_blk = tq // WINDOW
    first = pl.program_id(1) == 0
    rows = GQA_GROUP * WINDOW

    qi = lax.broadcasted_iota(jnp.int32, (rows, WINDOW), 0) % WINDOW
    kk = lax.broadcasted_iota(jnp.int32, (rows, WINDOW), 1)
    in_cur = kk <= qi
    scale = HEAD_DIM ** -0.5
    nt_dims = (((1,), (1,)), ((), ()))
    ones_blk = jnp.ones((2 * WINDOW, HEAD_DIM), BF16)

    for h in range(N_KV_HEADS):
        hs = slice(h * HEAD_DIM, (h + 1) * HEAD_DIM)
        sink = jnp.concatenate(
            [jnp.full((WINDOW, 1), sinks_ref[h * GQA_GROUP + g], F32) for g in range(GQA_GROUP)],
            axis=0)
        for n in range(n_blk):
            rs = slice(n * WINDOW, (n + 1) * WINDOW)
            q4 = jnp.concatenate(
                [q_ref[rs, (h * GQA_GROUP + g) * HEAD_DIM:(h * GQA_GROUP + g + 1) * HEAD_DIM]
                 for g in range(GQA_GROUP)], axis=0)
            if n == 0:
                k_prev, v_prev = kp_ref[:, hs], vp_ref[:, hs]
            else:
                ps = slice((n - 1) * WINDOW, n * WINDOW)
                k_prev, v_prev = k_ref[ps, hs], v_ref[ps, hs]
            k_cat = jnp.concatenate([k_prev, k_ref[rs, hs]], axis=0)
            v_cat = jnp.concatenate([v_prev, v_ref[rs, hs]], axis=0)
            s = lax.dot_general(q4, k_cat, nt_dims, preferred_element_type=F32)
            s_prev = s[:, :WINDOW]
            if n == 0:
                s_prev = jnp.where(first, -jnp.inf, s_prev)
            sc = jnp.where(in_cur, s[:, WINDOW:], s_prev) * scale
            m = jnp.maximum(jnp.max(sc, axis=-1, keepdims=True), sink)
            p = jnp.exp(sc - m)
            p_cat = jnp.concatenate(
                [jnp.where(in_cur, 0.0, p), jnp.where(in_cur, p, 0.0)], axis=1).astype(BF16)
            o_ext = jnp.dot(p_cat, jnp.concatenate([v_cat, ones_blk], axis=1),
                            preferred_element_type=F32)
            denom = o_ext[:, HEAD_DIM:] + jnp.exp(sink - m)
            o = o_ext[:, :HEAD_DIM] / denom
            for g in range(GQA_GROUP):
                cs = slice((h * GQA_GROUP + g) * HEAD_DIM, (h * GQA_GROUP + g + 1) * HEAD_DIM)
                y_s[rs, cs] = o[g * WINDOW:(g + 1) * WINDOW].astype(BF16)

    attn = jnp.dot(y_s[...], wa_ref[...], preferred_element_type=F32)
    merged = gb_ref[...].astype(F32) * attn + ma_ref[...].astype(F32)
    r = jnp.dot(merged.astype(BF16), wo_ref[...], preferred_element_type=F32)
    o_ref[...] = x_ref[...] + mod_ref[0, 2:3, :] * r


def _attn_merge(proj, merged_a, x, sinks, mod, w_attn, w_o, w_ffn_in, w_ffn_out, layer,
                batch, seq, tq):
    t, d = x.shape
    per_seq = seq // tq
    n_blk = tq // WINDOW
    row = lambda b, s: b * per_seq + s
    prev_blk = lambda b, s: jnp.maximum(row(b, s) * n_blk - 1, 0)
    fi_in, fi_out, fi_shape = _cast_specs(w_ffn_in.shape, layer, row, batch * per_seq)
    fo_in, fo_out, fo_shape = _cast_specs(w_ffn_out.shape, layer, row, batch * per_seq)
    return pl.pallas_call(
        _attn_kernel,
        grid=(batch, per_seq),
        in_specs=[
            pl.BlockSpec(memory_space=pltpu.SMEM),
            pl.BlockSpec((tq, d), lambda b, s: (row(b, s), OUT_Q)),
            pl.BlockSpec((tq, KV_WIDTH), lambda b, s: (row(b, s), OUT_K)),
            pl.BlockSpec((tq, KV_WIDTH), lambda b, s: (row(b, s), OUT_V)),
            pl.BlockSpec((WINDOW, KV_WIDTH), lambda b, s: (prev_blk(b, s), OUT_K)),
            pl.BlockSpec((WINDOW, KV_WIDTH), lambda b, s: (prev_blk(b, s), OUT_V)),
            pl.BlockSpec((tq, d), lambda b, s: (row(b, s), 0)),
            pl.BlockSpec((tq, d), lambda b, s: (row(b, s), OUT_GB)),
            pl.BlockSpec((tq, d), lambda b, s: (row(b, s), 0)),
            pl.BlockSpec((1, N_MOD, d), lambda b, s: (b, 0, 0)),
            _resident((d, d), lambda b, s: (0, 0)),
            _resident((d, d), lambda b, s: (0, 0)),
            fi_in,
            fo_in,
        ],
        out_specs=[pl.BlockSpec((tq, d), lambda b, s: (row(b, s), 0)), fi_out, fo_out],
        out_shape=[jax.ShapeDtypeStruct((t, d), F32), fi_shape, fo_shape],
        scratch_shapes=[pltpu.VMEM((tq, d), BF16)],
        compiler_params=_params("arbitrary", "arbitrary"),
        name="swa_merge_out_proj",
    )(sinks, proj, proj, proj, proj, proj, merged_a, proj, x, mod, w_attn, w_o,
      w_ffn_in, w_ffn_out)


def _ffn_kernel(x_ref, mod_ref, g_ref, wg_ref, wu_ref, wo_ref, fg_ref, o_ref, h_ref,
                *, rows, final_norm):
    acc_ref = o_ref
    tm, d = x_ref.shape
    th = wg_ref.shape[1]
    j = pl.program_id(1)

    @pl.when(j == 0)
    def _():
        _norm_mod_into(x_ref, h_ref, g_ref[...] * (1.0 + mod_ref[0, 4:5, :]), mod_ref[0, 3:4, :],
                       rows, also_zero=acc_ref)

    h = h_ref[...]
    acts = []
    for c0 in range(0, th, MXU_WIDTH):
        gate = jnp.dot(h, wg_ref[:, c0:c0 + MXU_WIDTH], preferred_element_type=F32)
        up = jnp.dot(h, wu_ref[:, c0:c0 + MXU_WIDTH], preferred_element_type=F32)
        acts.append((gate * _sigmoid_tanh(gate) * up).astype(BF16))
    act = jnp.concatenate(acts, axis=1)
    oc = 2 * MXU_WIDTH
    for c0 in range(0, d, oc):
        acc_ref[:, c0:c0 + oc] += jnp.dot(act, wo_ref[:, c0:c0 + oc],
                                           preferred_element_type=F32)

    @pl.when(j == pl.num_programs(1) - 1)
    def _():
        def body(i, carry):
            r = pl.ds(pl.multiple_of(i * rows, rows), rows)
            y = x_ref[r, :] + mod_ref[0, 5:6, :] * acc_ref[r, :]
            if final_norm:
                inv = lax.rsqrt(jnp.mean(y * y, axis=-1, keepdims=True) + EPS)
                y = (y * inv) * fg_ref[...]
            o_ref[r, :] = y
            return carry
        lax.fori_loop(0, tm // rows, body, 0)


def _ffn(x, mod, g, w_in, w_out, final_g, final_norm, seq, tm, th):
    t, d = x.shape
    hidden = w_out.shape[0]
    per_seq = seq // tm
    n_h = hidden // th
    return pl.pallas_call(
        functools.partial(_ffn_kernel, rows=min(tm, 128), final_norm=final_norm),
        grid=(t // tm, n_h),
        in_specs=[
            pl.BlockSpec((tm, d), lambda i, j: (i, 0)),
            pl.BlockSpec((1, N_MOD, d), lambda i, j: (i // per_seq, 0, 0)),
            pl.BlockSpec((1, d), lambda i, j: (0, 0)),
            pl.BlockSpec((d, th), lambda i, j: (0, j)),
            pl.BlockSpec((d, th), lambda i, j: (0, j + n_h)),
            pl.BlockSpec((th, d), lambda i, j: (j, 0)),
            pl.BlockSpec((1, d), lambda i, j: (0, 0)),
        ],
        out_specs=pl.BlockSpec((tm, d), lambda i, j: (i, 0)),
        out_shape=jax.ShapeDtypeStruct((t, d), F32),
        scratch_shapes=[pltpu.VMEM((tm, d), BF16)],
        compiler_params=_params("arbitrary", "arbitrary"),
        name="norm_swiglu",
    )(x, mod, g.reshape(1, d), w_in, w_in, w_out, final_g.reshape(1, d))


def kernel(x, c, ada_w, ada_b, norm1_g, w_in, b_in, conv_w, conv_b, lru_wa, lru_ba, lru_wx,
           lru_bx, lru_lambda, sinks, w_lru_out, w_attn_out, w_o, norm2_g, w_ffn_in, w_ffn_out,
           final_g):
    batch, seq, d = x.shape
    assert d == D_MODEL and seq % WINDOW == 0
    depth = ada_w.shape[0]
    t = batch * seq
    tm_proj = min(seq, TM_PROJ)
    tm_ffn = min(seq, TM_FFN)
    ts_lru = min(seq, TS_LRU)
    tq_attn = min(seq, TQ_ATTN)

    bg = jnp.concatenate([lru_ba.reshape(depth, LRU_HEADS, 1, LRU_HEAD_DIM),
                          lru_bx.reshape(depth, LRU_HEADS, 1, LRU_HEAD_DIM)], axis=-1)
    bg_hi = bg.astype(BF16)
    bg_lo = (bg - bg_hi.astype(F32)).astype(BF16)
    wg = jnp.concatenate(
        [jnp.concatenate([lru_wa, lru_wx], axis=-1).astype(BF16), bg_hi, bg_lo,
         jnp.zeros((depth, LRU_HEADS, LRU_HEAD_DIM - N_BIAS_ROWS, 2 * LRU_HEAD_DIM), BF16)],
        axis=2)
    b_in3 = b_in.reshape(depth, 1, C_IN)

    mod = _modulation(c, ada_w, ada_b)
    xf = x.reshape(t, d)
    w_in_l = w_in
    for l in range(depth):
        last = l == depth - 1
        proj = _in_proj(xf, mod[l], norm1_g[l], w_in_l, b_in3, l, seq, tm_proj)
        casts = ((w_attn_out, l), (w_o, l)) + (() if last else ((w_in, l + 1),))
        merged_a, w_attn_bf, w_o_bf, *w_in_next = _lru_branch(
            proj, conv_w[l], conv_b[l], wg[l], lru_lambda[l], w_lru_out, casts, l, batch, seq,
            ts_lru)
        xf, w_ffn_in_bf, w_ffn_out_bf = _attn_merge(
            proj, merged_a, xf, sinks[l], mod[l], w_attn_bf, w_o_bf, w_ffn_in, w_ffn_out, l,
            batch, seq, tq_attn)
        xf = _ffn(xf, mod[l], norm2_g[l], w_ffn_in_bf, w_ffn_out_bf, final_g, last, seq, tm_ffn,
                  FFN_TH)
        if w_in_next:
            w_in_l = w_in_next[0]
    return xf.reshape(batch, seq, d)
```

```python
import functools

import jax
import jax.numpy as jnp
from jax import lax
from jax.experimental import pallas as pl
from jax.experimental.pallas import tpu as pltpu

F32 = jnp.float32
BF16 = jnp.bfloat16

D_MODEL = 2048
LRU_HEADS = 16
LRU_HEAD_DIM = D_MODEL // LRU_HEADS
CONV_W = 4
LRU_C = 8.0
LOG2_E = 1.4426950408889634
HEAD_DIM = 128
N_Q_HEADS = 16
N_KV_HEADS = 4
GQA_GROUP = N_Q_HEADS // N_KV_HEADS
WINDOW = 128
KV_WIDTH = N_KV_HEADS * HEAD_DIM
N_MOD = 6
EPS = 1e-6
C_IN = 5 * D_MODEL + 2 * KV_WIDTH

PROJ_TN = 2 * KV_WIDTH
_BLK = D_MODEL // PROJ_TN
_KV_BLK = 2 * KV_WIDTH // PROJ_TN
OUT_U, OUT_GATE, OUT_Q, OUT_GA, OUT_GB = 0, 1, 2, 3, 4
OUT_K = 5 * D_MODEL // KV_WIDTH
OUT_V = OUT_K + 1
N_PROJ_BLK = C_IN // PROJ_TN

SUBLANES = 8
LANES = 128
MXU_WIDTH = 256
VMEM_LIMIT = 56 * 1024 * 1024
N_BIAS_ROWS = 2

TM_PROJ = 1024
TM_FFN = 1024
TS_LRU = 256
TQ_ATTN = 256
FFN_TH = 2 * MXU_WIDTH
ADALN_TN = 1024
SCAN_UNROLL = 8
RING_SLOTS = 3


def _src_col_block(j):
    return jnp.where(j < 3 * _BLK, j, jnp.where(j < 5 * _BLK, j + _KV_BLK, j - 2 * _BLK))


def _params(*sem):
    return pltpu.CompilerParams(dimension_semantics=sem, vmem_limit_bytes=VMEM_LIMIT)


def _cast_plan(rows, n_steps):
    k = n_steps
    while rows % k or (rows // k) % (2 * SUBLANES):
        k -= 1
    return rows // k, k


def _cast_specs(shape, layer, step_of, n_steps):
    rows, cols = shape[1:]
    blk, n_blk = _cast_plan(rows, n_steps)
    walk = lambda *g: jnp.minimum(step_of(*g), n_blk - 1)
    return (pl.BlockSpec((None, blk, cols), lambda *g: (layer, walk(*g), 0)),
            pl.BlockSpec((blk, cols), lambda *g: (walk(*g), 0)),
            jax.ShapeDtypeStruct((rows, cols), BF16))


def _resident(shape, index_map):
    return pl.BlockSpec(shape, index_map, pipeline_mode=pl.Buffered(1))


def _norm_mod_into(x_ref, h_ref, gain, shift, rows, also_zero=None):
    tm, d = x_ref.shape
    group = 2 * SUBLANES

    def body(i, carry):
        for q in range(rows // group):
            r = pl.ds(pl.multiple_of(i * rows + q * group, group), group)
            x = x_ref[r, :]
            inv = lax.rsqrt(jnp.mean(x * x, axis=-1, keepdims=True) + EPS)
            h_ref[r, :] = ((x * inv) * gain + shift).astype(BF16)
            if also_zero is not None:
                also_zero[r, :] = jnp.zeros((group, d), F32)
        return carry

    lax.fori_loop(0, tm // rows, body, 0)


_SQRT_2_OVER_PI = 0.7978845608028654


def _gelu_tanh(x):
    hx = 0.5 * x
    return hx + hx * jnp.tanh(x * (_SQRT_2_OVER_PI + (_SQRT_2_OVER_PI * 0.044715) * (x * x)))


def _sigmoid_tanh(x):
    return 0.5 * jnp.tanh(0.5 * x) + 0.5


def _softplus(z):
    return jnp.maximum(z, 0.0) + jnp.log1p(jnp.exp(-jnp.abs(z)))


def _mod_kernel(c_ref, w_ref, b_ref, o_ref):
    c = c_ref[...]
    act = (c * jax.nn.sigmoid(c)).astype(BF16)
    o_ref[0] = jnp.dot(act, w_ref[0].astype(BF16), preferred_element_type=F32) + b_ref[0]


def _modulation(c, ada_w, ada_b):
    depth, d, n = ada_w.shape
    tn = ADALN_TN
    b = c.shape[0]
    rows = -(-b // SUBLANES) * SUBLANES
    c_pad = jnp.zeros((rows, d), F32).at[:b].set(c)
    out = pl.pallas_call(
        _mod_kernel,
        grid=(depth, n // tn),
        in_specs=[
            pl.BlockSpec((rows, d), lambda l, j: (0, 0)),
            pl.BlockSpec((1, d, tn), lambda l, j: (l, 0, j)),
            pl.BlockSpec((1, 1, tn), lambda l, j: (l, 0, j)),
        ],
        out_specs=pl.BlockSpec((1, rows, tn), lambda l, j: (l, 0, j)),
        out_shape=jax.ShapeDtypeStruct((depth, rows, n), F32),
        compiler_params=_params("arbitrary", "arbitrary"),
        name="adaln_modulation",
    )(c_pad, ada_w, ada_b.reshape(depth, 1, n))
    return out[:, :b].reshape(depth, b, N_MOD, d)


def _inproj_kernel(x_ref, mod_ref, g_ref, w_ref, b_ref, o_ref, h_ref, *ring, rows, layer):
    j = pl.program_id(1)

    if ring:
        wbuf, sem = ring
        n_j = pl.num_programs(1)
        n_steps = pl.num_programs(0) * n_j
        step = pl.program_id(0) * n_j + j

        def weight_copy(s):
            col = pl.multiple_of(_src_col_block(s % n_j) * PROJ_TN, PROJ_TN)
            slot = s % RING_SLOTS
            return pltpu.make_async_copy(w_ref.at[layer, :, pl.ds(col, PROJ_TN)],
                                         wbuf.at[slot], sem.at[slot])

        @pl.when(step == 0)
        def _():
            for s in range(RING_SLOTS - 1):
                weight_copy(s).start()

        @pl.when(step + (RING_SLOTS - 1) < n_steps)
        def _():
            weight_copy(step + (RING_SLOTS - 1)).start()

        weight_copy(step).wait()
        w_tile = wbuf.at[step % RING_SLOTS]
    else:
        w_tile = w_ref

    @pl.when(j == 0)
    def _():
        _norm_mod_into(x_ref, h_ref, g_ref[...] * (1.0 + mod_ref[0, 1:2, :]), mod_ref[0, 0:1, :],
                       rows)

    def project(act):
        z = jnp.dot(h_ref[...], w_tile[...].astype(BF16), preferred_element_type=F32) + b_ref[...]
        o_ref[...] = act(z).astype(o_ref.dtype)

    blk = j // _BLK
    is_gelu = blk == OUT_GATE
    is_sigmoid = jnp.logical_or(blk == OUT_GA, blk == OUT_GB)

    @pl.when(is_gelu)
    def _():
        project(_gelu_tanh)

    @pl.when(is_sigmoid)
    def _():
        project(_sigmoid_tanh)

    @pl.when(jnp.logical_not(jnp.logical_or(is_gelu, is_sigmoid)))
    def _():
        project(lambda z: z)


def _in_proj(x, mod, g, w, b, layer, seq, tm):
    t, d = x.shape
    tn = PROJ_TN
    per_seq = seq // tm
    scratch = [pltpu.VMEM((tm, d), BF16)]
    if w.ndim == 3:
        w_spec = pl.BlockSpec(memory_space=pl.ANY)
        scratch += [pltpu.VMEM((RING_SLOTS, d, tn), F32), pltpu.SemaphoreType.DMA((RING_SLOTS,))]
    else:
        w_spec = pl.BlockSpec((d, tn), lambda i, j: (0, _src_col_block(j)))
    return pl.pallas_call(
        functools.partial(_inproj_kernel, rows=min(tm, 128), layer=layer),
        grid=(t // tm, N_PROJ_BLK),
        in_specs=[
            pl.BlockSpec((tm, d), lambda i, j: (i, 0)),
            pl.BlockSpec((1, N_MOD, d), lambda i, j: (i // per_seq, 0, 0)),
            pl.BlockSpec((1, d), lambda i, j: (0, 0)),
            w_spec,
            pl.BlockSpec((None, 1, tn), lambda i, j: (layer, 0, _src_col_block(j))),
        ],
        out_specs=pl.BlockSpec((tm, tn), lambda i, j: (i, j)),
        out_shape=jax.ShapeDtypeStruct((t, C_IN), BF16),
        scratch_shapes=scratch,
        compiler_params=_params("arbitrary", "arbitrary"),
        name="norm_in_proj",
    )(x, mod, g.reshape(1, d), w, b)


def _lru_kernel(u_ref, gate_ref, mg_ref, convw_ref, convb_ref, wg_ref, lam_ref, wout_ref,
                *rest, group, per_seq, n_cast):
    w32_refs, o_ref = rest[:n_cast], rest[n_cast]
    w16_refs = rest[n_cast + 1:2 * n_cast + 1]
    ubuf, a_s, b_s, y_s, hcarry, wout_bf = rest[2 * n_cast + 1:]
    for w32_ref, w16_ref in zip(w32_refs, w16_refs):
        w16_ref[...] = w32_ref[...].astype(BF16)
    ts = u_ref.shape[0]
    sub = ts // SUBLANES
    pitch = sub + 4
    halo = SUBLANES
    g_idx = pl.program_id(0)
    first = g_idx % per_seq == 0

    @pl.when(g_idx == 0)
    def _():
        ubuf[:, ts:ts + halo, :] = jnp.zeros((LRU_HEADS, halo, LANES), F32)
        hcarry[...] = jnp.zeros_like(hcarry)
        y_s[...] = jnp.zeros_like(y_s)
        wout_bf[...] = wout_ref[...].astype(BF16)

    ubuf[:, 0:halo, :] = jnp.where(first, 0.0, ubuf[:, ts:ts + halo, :])
    hcarry[...] = jnp.where(first, 0.0, hcarry[...])

    c_softplus = LRU_C * _softplus(-lam_ref[...])
    bias_cols = (lax.broadcasted_iota(jnp.int32, (ts, LRU_HEAD_DIM), 1)
                 < N_BIAS_ROWS).astype(BF16)
    out_cols = MXU_WIDTH
    heads_per_chunk = out_cols // LRU_HEAD_DIM

    for hd in range(LRU_HEADS):
        if hd % heads_per_chunk == 0:
            c0 = (hd // heads_per_chunk) * out_cols
            out = jnp.dot(y_s[...], wout_bf[:, c0:c0 + out_cols], preferred_element_type=F32)
            o_ref[:, c0:c0 + out_cols] = (
                mg_ref[:, c0:c0 + out_cols].astype(F32) * out).astype(o_ref.dtype)
        cs = slice(hd * LRU_HEAD_DIM, (hd + 1) * LRU_HEAD_DIM)
        ubuf[hd, halo:halo + ts, :] = u_ref[:, cs].astype(F32)
        uc = convb_ref[:, cs]
        for k in range(CONV_W):
            off = halo - (CONV_W - 1) + k
            uc = uc + convw_ref[k:k + 1, cs] * ubuf[hd, off:off + ts, :]
        g = jnp.dot(jnp.concatenate([uc.astype(BF16), bias_cols], axis=1), wg_ref[hd],
                    preferred_element_type=F32)
        r = _sigmoid_tanh(g[:, :LRU_HEAD_DIM])
        i = _sigmoid_tanh(g[:, LRU_HEAD_DIM:])
        a = jnp.exp2(r * (c_softplus[:, cs] * -LOG2_E))
        one_m_a2 = (1.0 + a * a) * jnp.tanh(r * c_softplus[:, cs])
        beta = jnp.where(one_m_a2 > 0.0, one_m_a2 * lax.rsqrt(one_m_a2), 0.0)
        b = beta * (i * uc)
        for s in range(SUBLANES):
            a_s[hd, s * pitch:s * pitch + sub, :] = a[s * sub:(s + 1) * sub, :]
            b_s[hd, s * pitch:s * pitch + sub, :] = b[s * sub:(s + 1) * sub, :]

    row = lax.broadcasted_iota(jnp.int32, (SUBLANES, LANES), 0)
    zeros = jnp.zeros((SUBLANES, LANES), F32)
    ones = jnp.ones((SUBLANES, LANES), F32)

    for h0 in range(0, LRU_HEADS, group):
        heads = range(h0, h0 + group)

        def strided(ref, hd, m):
            return ref[hd, pl.ds(m, SUBLANES, stride=pitch), :]

        def pass1(m, carry):
            out = []
            for n, hd in enumerate(heads):
                a_m = strided(a_s, hd, m)
                out += [a_m * carry[2 * n] + strided(b_s, hd, m), carry[2 * n + 1] * a_m]
            return tuple(out)

        ends = lax.fori_loop(0, sub, pass1, (zeros, ones) * group, unroll=SCAN_UNROLL)

        starts = []
        for n, hd in enumerate(heads):
            e, q = ends[2 * n], ends[2 * n + 1]
            for d in (1, 2, 4):
                keep = row >= d
                e_prev = pltpu.roll(e, d, 0)
                q_prev = pltpu.roll(q, d, 0)
                e = jnp.where(keep, q * e_prev, 0.0) + e
                q = jnp.where(keep, q * q_prev, q)
            cs = slice(hd * LRU_HEAD_DIM, (hd + 1) * LRU_HEAD_DIM)
            h_in = hcarry[:, cs]
            end_state = e + q * h_in
            hcarry[:, cs] = end_state[SUBLANES - 1:SUBLANES, :]
            starts.append(jnp.where(row == 0, h_in, pltpu.roll(end_state, 1, 0)))

        def pass2(m, carry):
            out = []
            for n, hd in enumerate(heads):
                h = strided(a_s, hd, m) * carry[n] + strided(b_s, hd, m)
                b_s[hd, pl.ds(m, SUBLANES, stride=pitch), :] = h
                out.append(h)
            return tuple(out)

        lax.fori_loop(0, sub, pass2, tuple(starts), unroll=SCAN_UNROLL)

    for hd in range(LRU_HEADS):
        cs = slice(hd * LRU_HEAD_DIM, (hd + 1) * LRU_HEAD_DIM)
        h = jnp.concatenate(
            [b_s[hd, s * pitch:s * pitch + sub, :] for s in range(SUBLANES)], axis=0)
        y_s[:, cs] = (h * gate_ref[:, cs].astype(F32)).astype(BF16)


def _lru_branch(proj, conv_w, conv_b, wg, lam, w_out, casts, layer, batch, seq, ts):
    t = proj.shape[0]
    d = D_MODEL
    per_seq = seq // ts
    n_tiles = batch * per_seq
    pitch = ts // SUBLANES + 4
    cur = lambda g: jnp.minimum(g, n_tiles - 1)
    prev = lambda g: jnp.maximum(g - 1, 0)
    cast_specs = [_cast_specs(w.shape, wl, lambda g: g, n_tiles + 1) for w, wl in casts]
    return pl.pallas_call(
        functools.partial(_lru_kernel, group=8, per_seq=per_seq, n_cast=len(casts)),
        grid=(n_tiles + 1,),
        in_specs=[
            pl.BlockSpec((ts, d), lambda g: (cur(g), OUT_U)),
            pl.BlockSpec((ts, d), lambda g: (cur(g), OUT_GATE)),
            pl.BlockSpec((ts, d), lambda g: (prev(g), OUT_GA)),
            _resident((CONV_W, d), lambda g: (0, 0)),
            _resident((1, d), lambda g: (0, 0)),
            _resident((LRU_HEADS, 2 * LRU_HEAD_DIM, 2 * LRU_HEAD_DIM), lambda g: (0, 0, 0)),
            _resident((1, d), lambda g: (0, 0)),
            _resident((None, d, d), lambda g: (layer, 0, 0)),
            *[c[0] for c in cast_specs],
        ],
        out_specs=[pl.BlockSpec((ts, d), lambda g: (prev(g), 0)), *[c[1] for c in cast_specs]],
        out_shape=[jax.ShapeDtypeStruct((t, d), BF16), *[c[2] for c in cast_specs]],
        scratch_shapes=[
            pltpu.VMEM((LRU_HEADS, ts + SUBLANES, LANES), F32),
            pltpu.VMEM((LRU_HEADS, SUBLANES * pitch, LANES), F32),
            pltpu.VMEM((LRU_HEADS, SUBLANES * pitch, LANES), F32),
            pltpu.VMEM((ts, d), BF16),
            pltpu.VMEM((1, d), F32),
            pltpu.VMEM((d, d), BF16),
        ],
        compiler_params=_params("arbitrary"),
        name="rglru_branch",
    )(proj, proj, proj, conv_w, conv_b.reshape(1, d), wg, lam.reshape(1, d), w_out,
      *[w for w, _ in casts])


def _attn_kernel(sinks_ref, q_ref, k_ref, v_ref, kp_ref, vp_ref, ma_ref, gb_ref, x_ref,
                 mod_ref, wa_ref, wo_ref, wfi32_ref, wfo32_ref, o_ref, wfi16_ref, wfo16_ref, y_s):
    wfi16_ref[...] = wfi32_ref[...].astype(BF16)
    wfo16_ref[...] = wfo32_ref[...].astype(BF16)

    tq = q_ref.shape[0]
    n_blk = tq // WINDOW
    first = pl.program_id(1) == 0
    rows = GQA_GROUP * WINDOW

    qi = lax.broadcasted_iota(jnp.int32, (rows, WINDOW), 0) % WINDOW
    kk = lax.broadcasted_iota(jnp.int32, (rows, WINDOW), 1)
    in_cur = kk <= qi
    scale = HEAD_DIM ** -0.5
    nt_dims = (((1,), (1,)), ((), ()))
    ones_blk = jnp.ones((2 * WINDOW, HEAD_DIM), BF16)

    for h in range(N_KV_HEADS):
        hs = slice(h * HEAD_DIM, (h + 1) * HEAD_DIM)
        sink = jnp.concatenate(
            [jnp.full((WINDOW, 1), sinks_ref[h * GQA_GROUP + g], F32) for g in range(GQA_GROUP)],
            axis=0)
        for n in range(n_blk):
            rs = slice(n * WINDOW, (n + 1) * WINDOW)
            q4 = jnp.concatenate(
                [q_ref[rs, (h * GQA_GROUP + g) * HEAD_DIM:(h * GQA_GROUP + g + 1) * HEAD_DIM]
                 for g in range(GQA_GROUP)], axis=0)
            if n == 0:
                k_prev, v_prev = kp_ref[:, hs], vp_ref[:, hs]
            else:
                ps = slice((n - 1) * WINDOW, n * WINDOW)
                k_prev, v_prev = k_ref[ps, hs], v_ref[ps, hs]
            k_cat = jnp.concatenate([k_prev, k_ref[rs, hs]], axis=0)
            v_cat = jnp.concatenate([v_prev, v_ref[rs, hs]], axis=0)
            s = lax.dot_general(q4, k_cat, nt_dims, preferred_element_type=F32)
            s_prev = s[:, :WINDOW]
            if n == 0:
                s_prev = jnp.where(first, -jnp.inf, s_prev)
            sc = jnp.where(in_cur, s[:, WINDOW:], s_prev) * scale
            m = jnp.maximum(jnp.max(sc, axis=-1, keepdims=True), sink)
            p = jnp.exp(sc - m)
            p_cat = jnp.concatenate(
                [jnp.where(in_cur, 0.0, p), jnp.where(in_cur, p, 0.0)], axis=1).astype(BF16)
            o_ext = jnp.dot(p_cat, jnp.concatenate([v_cat, ones_blk], axis=1),
                            preferred_element_type=F32)
            denom = o_ext[:, HEAD_DIM:] + jnp.exp(sink - m)
            o = o_ext[:, :HEAD_DIM] / denom
            for g in range(GQA_GROUP):
                cs = slice((h * GQA_GROUP + g) * HEAD_DIM, (h * GQA_GROUP + g + 1) * HEAD_DIM)
                y_s[rs, cs] = o[g * WINDOW:(g + 1) * WINDOW].astype(BF16)

    attn = jnp.dot(y_s[...], wa_ref[...], preferred_element_type=F32)
    merged = gb_ref[...].astype(F32) * attn + ma_ref[...].astype(F32)
    r = jnp.dot(merged.astype(BF16), wo_ref[...], preferred_element_type=F32)
    o_ref[...] = x_ref[...] + mod_ref[0, 2:3, :] * r


def _attn_merge(proj, merged_a, x, sinks, mod, w_attn, w_o, w_ffn_in, w_ffn_out, layer,
                batch, seq, tq):
    t, d = x.shape
    per_seq = seq // tq
    n_blk = tq // WINDOW
    row = lambda b, s: b * per_seq + s
    prev_blk = lambda b, s: jnp.maximum(row(b, s) * n_blk - 1, 0)
    fi_in, fi_out, fi_shape = _cast_specs(w_ffn_in.shape, layer, row, batch * per_seq)
    fo_in, fo_out, fo_shape = _cast_specs(w_ffn_out.shape, layer, row, batch * per_seq)
    return pl.pallas_call(
        _attn_kernel,
        grid=(batch, per_seq),
        in_specs=[
            pl.BlockSpec(memory_space=pltpu.SMEM),
            pl.BlockSpec((tq, d), lambda b, s: (row(b, s), OUT_Q)),
            pl.BlockSpec((tq, KV_WIDTH), lambda b, s: (row(b, s), OUT_K)),
            pl.BlockSpec((tq, KV_WIDTH), lambda b, s: (row(b, s), OUT_V)),
            pl.BlockSpec((WINDOW, KV_WIDTH), lambda b, s: (prev_blk(b, s), OUT_K)),
            pl.BlockSpec((WINDOW, KV_WIDTH), lambda b, s: (prev_blk(b, s), OUT_V)),
            pl.BlockSpec((tq, d), lambda b, s: (row(b, s), 0)),
            pl.BlockSpec((tq, d), lambda b, s: (row(b, s), OUT_GB)),
            pl.BlockSpec((tq, d), lambda b, s: (row(b, s), 0)),
            pl.BlockSpec((1, N_MOD, d), lambda b, s: (b, 0, 0)),
            _resident((d, d), lambda b, s: (0, 0)),
            _resident((d, d), lambda b, s: (0, 0)),
            fi_in,
            fo_in,
        ],
        out_specs=[pl.BlockSpec((tq, d), lambda b, s: (row(b, s), 0)), fi_out, fo_out],
        out_shape=[jax.ShapeDtypeStruct((t, d), F32), fi_shape, fo_shape],
        scratch_shapes=[pltpu.VMEM((tq, d), BF16)],
        compiler_params=_params("arbitrary", "arbitrary"),
        name="swa_merge_out_proj",
    )(sinks, proj, proj, proj, proj, proj, merged_a, proj, x, mod, w_attn, w_o,
      w_ffn_in, w_ffn_out)


def _ffn_kernel(x_ref, mod_ref, g_ref, wg_ref, wu_ref, wo_ref, fg_ref, o_ref, h_ref,
                *, rows, final_norm):
    acc_ref = o_ref
    tm, d = x_ref.shape
    th = wg_ref.shape[1]
    j = pl.program_id(1)

    @pl.when(j == 0)
    def _():
        _norm_mod_into(x_ref, h_ref, g_ref[...] * (1.0 + mod_ref[0, 4:5, :]), mod_ref[0, 3:4, :],
                       rows, also_zero=acc_ref)

    h = h_ref[...]
    acts = []
    for c0 in range(0, th, MXU_WIDTH):
        gate = jnp.dot(h, wg_ref[:, c0:c0 + MXU_WIDTH], preferred_element_type=F32)
        up = jnp.dot(h, wu_ref[:, c0:c0 + MXU_WIDTH], preferred_element_type=F32)
        acts.append((gate * _sigmoid_tanh(gate) * up).astype(BF16))
    act = jnp.concatenate(acts, axis=1)
    oc = 2 * MXU_WIDTH
    for c0 in range(0, d, oc):
        acc_ref[:, c0:c0 + oc] += jnp.dot(act, wo_ref[:, c0:c0 + oc],
                                           preferred_element_type=F32)

    @pl.when(j == pl.num_programs(1) - 1)
    def _():
        def body(i, carry):
            r = pl.ds(pl.multiple_of(i * rows, rows), rows)
            y = x_ref[r, :] + mod_ref[0, 5:6, :] * acc_ref[r, :]
            if final_norm:
                inv = lax.rsqrt(jnp.mean(y * y, axis=-1, keepdims=True) + EPS)
                y = (y * inv) * fg_ref[...]
            o_ref[r, :] = y
            return carry
        lax.fori_loop(0, tm // rows, body, 0)


def _ffn(x, mod, g, w_in, w_out, final_g, final_norm, seq, tm, th):
    t, d = x.shape
    hidden = w_out.shape[0]
    per_seq = seq // tm
    n_h = hidden // th
    return pl.pallas_call(
        functools.partial(_ffn_kernel, rows=min(tm, 128), final_norm=final_norm),
        grid=(t // tm, n_h),
        in_specs=[
            pl.BlockSpec((tm, d), lambda i, j: (i, 0)),
            pl.BlockSpec((1, N_MOD, d), lambda i, j: (i // per_seq, 0, 0)),
            pl.BlockSpec((1, d), lambda i, j: (0, 0)),
            pl.BlockSpec((d, th), lambda i, j: (0, j)),
            pl.BlockSpec((d, th), lambda i, j: (0, j + n_h)),
            pl.BlockSpec((th, d), lambda i, j: (j, 0)),
            pl.BlockSpec((1, d), lambda i, j: (0, 0)),
        ],
        out_specs=pl.BlockSpec((tm, d), lambda i, j: (i, 0)),
        out_shape=jax.ShapeDtypeStruct((t, d), F32),
        scratch_shapes=[pltpu.VMEM((tm, d), BF16)],
        compiler_params=_params("arbitrary", "arbitrary"),
        name="norm_swiglu",
    )(x, mod, g.reshape(1, d), w_in, w_in, w_out, final_g.reshape(1, d))


def kernel(x, c, ada_w, ada_b, norm1_g, w_in, b_in, conv_w, conv_b, lru_wa, lru_ba, lru_wx,
           lru_bx, lru_lambda, sinks, w_lru_out, w_attn_out, w_o, norm2_g, w_ffn_in, w_ffn_out,
           final_g):
    batch, seq, d = x.shape
    assert d == D_MODEL and seq % WINDOW == 0
    depth = ada_w.shape[0]
    t = batch * seq
    tm_proj = min(seq, TM_PROJ)
    tm_ffn = min(seq, TM_FFN)
    ts_lru = min(seq, TS_LRU)
    tq_attn = min(seq, TQ_ATTN)

    bg = jnp.concatenate([lru_ba.reshape(depth, LRU_HEADS, 1, LRU_HEAD_DIM),
                          lru_bx.reshape(depth, LRU_HEADS, 1, LRU_HEAD_DIM)], axis=-1)
    bg_hi = bg.astype(BF16)
    bg_lo = (bg - bg_hi.astype(F32)).astype(BF16)
    wg = jnp.concatenate(
        [jnp.concatenate([lru_wa, lru_wx], axis=-1).astype(BF16), bg_hi, bg_lo,
         jnp.zeros((depth, LRU_HEADS, LRU_HEAD_DIM - N_BIAS_ROWS, 2 * LRU_HEAD_DIM), BF16)],
        axis=2)
    b_in3 = b_in.reshape(depth, 1, C_IN)

    mod = _modulation(c, ada_w, ada_b)
    xf = x.reshape(t, d)
    w_in_l = w_in
    for l in range(depth):
        last = l == depth - 1
        proj = _in_proj(xf, mod[l], norm1_g[l], w_in_l, b_in3, l, seq, tm_proj)
        casts = ((w_attn_out, l), (w_o, l)) + (() if last else ((w_in, l + 1),))
        merged_a, w_attn_bf, w_o_bf, *w_in_next = _lru_branch(
            proj, conv_w[l], conv_b[l], wg[l], lru_lambda[l], w_lru_out, casts, l, batch, seq,
            ts_lru)
        xf, w_ffn_in_bf, w_ffn_out_bf = _attn_merge(
            proj, merged_a, xf, sinks[l], mod[l], w_attn_bf, w_o_bf, w_ffn_in, w_ffn_out, l,
            batch, seq, tq_attn)
        xf = _ffn(xf, mod[l], norm2_g[l], w_ffn_in_bf, w_ffn_out_bf, final_g, last, seq, tm_ffn,
                  FFN_TH)
        if w_in_next:
            w_in_l = w_in_next[0]
    return xf.reshape(batch, seq, d)
```
